```python
import math
import jax
import jax.numpy as jnp
from jax import lax
import numpy as np

D_MODEL = 2048
BATCH = 2
SEQ = 4096
DEPTH = 2
DEC_BATCH = 128
DEC_SEQ = 8
PAST_LEN = 16384
PAGE_SIZE = 128

N_EVEN = (DEPTH + 1) // 2
N_ODD = DEPTH // 2
MIX_HALF = D_MODEL // 2
A_DIM = 64
A_HEADS = MIX_HALF // (2 * A_DIM)
A_KV_HEADS = 2
A_GROUP = A_HEADS // A_KV_HEADS
B_WINDOWS = (2, 4, 8, 16)
B_GROUPS = len(B_WINDOWS)
B_WIDTH = MIX_HALF
B_GROUP_DIM = B_WIDTH // B_GROUPS
B_HIST = max(B_WINDOWS) - 1
C_NOPE = 128
C_ROPE = 64
C_V = 128
C_LATENT = 512
C_HEADS = MIX_HALF // C_V
C_QK = C_NOPE + C_ROPE
ROPE_THETA = 10000.0
D_WIDTH = MIX_HALF
D_GROUPS = 4
D_GROUP_DIM = D_WIDTH // D_GROUPS
D_CHUNK = 128
MEM_LEN = 256
X_HEADS = 4
X_DIM = 128
X_WIDTH = X_HEADS * X_DIM
N_GROUPS = 4
EXPERTS_PER_GROUP = 8
N_EXPERTS = N_GROUPS * EXPERTS_PER_GROUP
D_EXPERT = 512
TOP_K = 2
Q_BLOCK = 128
EPS = 1e-6
NEG_INF = -1e30
A_QCOLS = A_HEADS * 2 * A_DIM
A_KCOLS = A_KV_HEADS * 2 * A_DIM
L0_COLS = A_QCOLS + 2 * A_KCOLS + B_WIDTH
L0_OUT = A_HEADS * 2 * A_DIM + B_WIDTH
C_QCOLS = C_HEADS * C_QK
L1_COLS = C_QCOLS + C_LATENT + C_ROPE + 2 * D_WIDTH
L1_OUT = C_HEADS * C_V + D_WIDTH

kernel_name = 'hybrid_diffattn_pool_mla_gmlp_hmoe_step'


def _rms(x, g):
    xf = x.astype(jnp.float32)
    y = xf * lax.rsqrt(jnp.mean(xf * xf, axis=-1, keepdims=True) + EPS)
    return (y * g.astype(jnp.float32)).astype(x.dtype)


def _masked_softmax(s, mask):
    return jax.nn.softmax(jnp.where(mask, s, NEG_INF), axis=-1)


def _rope(x, pos):
    half = C_ROPE // 2
    inv = ROPE_THETA ** (-jnp.arange(half, dtype=jnp.float32) * 2.0 / C_ROPE)
    ang = pos.astype(jnp.float32)[:, None] * inv[None, :]
    cos = jnp.cos(ang)[:, None, :]
    sin = jnp.sin(ang)[:, None, :]
    xf = x.astype(jnp.float32)
    x1, x2 = xf[..., :half], xf[..., half:]
    return jnp.concatenate([x1 * cos - x2 * sin, x2 * cos + x1 * sin], axis=-1).astype(x.dtype)


def _diff_lambda(lam_p, lam_init):
    lp = lam_p.astype(jnp.float32)
    return jnp.exp(jnp.sum(lp[0] * lp[1])) - jnp.exp(jnp.sum(lp[2] * lp[3])) + lam_init


def _split_even(proj, gq, gk):
    b, t, _ = proj.shape
    q = _rms(proj[..., :A_QCOLS].reshape(b, t, A_HEADS, 2, A_DIM), gq)
    k = _rms(proj[..., A_QCOLS:A_QCOLS + A_KCOLS].reshape(b, t, A_KV_HEADS, 2, A_DIM), gk)
    v = proj[..., A_QCOLS + A_KCOLS:A_QCOLS + 2 * A_KCOLS].reshape(b, t, A_KV_HEADS, 2 * A_DIM)
    hb = proj[..., A_QCOLS + 2 * A_KCOLS:]
    return q, k, v, hb


def _diff_core(q, k, v, mask, lam, g_out, lam_init):
    b, t = q.shape[:2]
    qg = q.reshape(b, t, A_KV_HEADS, A_GROUP, 2, A_DIM)
    s = jnp.einsum('btkgmd,blkmd->bkgmtl', qg, k, preferred_element_type=jnp.float32) * (A_DIM ** -0.5)
    p = _masked_softmax(s, mask)
    a = (p[:, :, :, 0] - lam * p[:, :, :, 1]).astype(v.dtype)
    o = jnp.einsum('bkgtl,blkd->btkgd', a, v).reshape(b, t, A_HEADS, 2 * A_DIM)
    return (_rms(o, g_out) * (1.0 - lam_init)).reshape(b, t, A_HEADS * 2 * A_DIM)


def _diff_prompt(q, k, v, lam, g_out, lam_init):
    b, s = q.shape[:2]
    nb = s // Q_BLOCK
    qb = jnp.moveaxis(q.reshape(b, nb, Q_BLOCK, A_HEADS, 2, A_DIM), 1, 0)
    kpos = jnp.arange(s)

    def block(args):
        i, qi = args
        qpos = i * Q_BLOCK + jnp.arange(Q_BLOCK)
        return _diff_core(qi, k, v, qpos[:, None] >= kpos[None, :], lam, g_out, lam_init)

    o = lax.map(block, (jnp.arange(nb), qb))
    return jnp.moveaxis(o, 0, 1).reshape(b, s, A_HEADS * 2 * A_DIM)


def _diff_sample(q, k_new, v_new, pool_k, pool_v, li, page_table, lam, g_out, lam_init):
    t = q.shape[1]
    past = page_table.shape[1] * PAGE_SIZE
    mask = (past + jnp.arange(t))[:, None] >= jnp.arange(past + t)[None, :]

    def one(args):
        qb, kb, vb, pages = args
        kp = pool_k[li, pages].reshape(past, A_KV_HEADS, 2, A_DIM)
        vp = pool_v[li, pages].reshape(past, A_KV_HEADS, 2 * A_DIM)
        k_all = jnp.concatenate([kp, kb], axis=0)[None]
        v_all = jnp.concatenate([vp, vb], axis=0)[None]
        return _diff_core(qb[None], k_all, v_all, mask, lam, g_out, lam_init)[0]

    return lax.map(one, (q, k_new, v_new, page_table))


def _multi_pool(ext, start_pos, w_lin, scale):
    b, n, c = ext.shape
    t = n - B_HIST
    xf = ext.astype(jnp.float32)
    cs = jnp.concatenate([jnp.zeros((b, 1, c), jnp.float32), jnp.cumsum(xf, axis=1)], axis=1)
    end = cs[:, B_HIST + 1:]
    pos = start_pos + jnp.arange(t)
    means = []
    for g, w in enumerate(B_WINDOWS):
        lo, hi = g * B_GROUP_DIM, (g + 1) * B_GROUP_DIM
        start = cs[:, B_HIST + 1 - w:B_HIST + 1 - w + t, lo:hi]
        cnt = jnp.minimum(pos + 1, w).astype(jnp.float32)[None, :, None]
        means.append((end[..., lo:hi] - start) / cnt)
    d = (jnp.concatenate(means, axis=-1) - xf[:, B_HIST:]).astype(ext.dtype)
    y = jnp.einsum('btgc,gcd->btgd', d.reshape(b, t, B_GROUPS, B_GROUP_DIM), w_lin)
    return y.reshape(b, t, B_WIDTH) * scale


def _split_odd(proj, g_lat, g_v):
    b, t, _ = proj.shape
    o1 = C_QCOLS
    o2 = o1 + C_LATENT
    o3 = o2 + C_ROPE
    o4 = o3 + D_WIDTH
    q = proj[..., :o1].reshape(b, t, C_HEADS, C_QK)
    c = _rms(proj[..., o1:o2], g_lat)
    kpe = proj[..., o2:o3]
    u = jax.nn.gelu(proj[..., o3:o4])
    v = jax.nn.gelu(proj[..., o4:]).reshape(b, t, D_GROUPS, D_GROUP_DIM)
    v = _rms(v, g_v).reshape(b, t, D_WIDTH)
    return q, c, kpe, u, v


def _mla_queries(q, qpos, gq):
    q = _rms(q, gq)
    return jnp.concatenate([q[..., :C_NOPE], _rope(q[..., C_NOPE:], qpos)], axis=-1)


def _mla_keys(c, kpe, kpos, w_uk, gk):
    b, l = c.shape[:2]
    k_nope = jnp.einsum('blc,chd->blhd', c, w_uk)
    k_pe = jnp.broadcast_to(kpe[:, :, None, :], (b, l, C_HEADS, C_ROPE))
    k = _rms(jnp.concatenate([k_nope, k_pe], axis=-1), gk)
    return jnp.concatenate([k[..., :C_NOPE], _rope(k[..., C_NOPE:], kpos)], axis=-1)


def _mla_core(q, k, c, w_uv, mask):
    b, t = q.shape[:2]
    s = jnp.einsum('bthd,blhd->bhtl', q, k, preferred_element_type=jnp.float32) * (C_QK ** -0.5)
    p = _masked_softmax(s, mask).astype(c.dtype)
    ctx = jnp.einsum('bhtl,blc->bthc', p, c)
    return jnp.einsum('bthc,chd->bthd', ctx, w_uv).reshape(b, t, C_HEADS * C_V)


def _mla_prompt(q, c, kpe, w_uk, w_uv, gk):
    b, s = q.shape[:2]
    kpos = jnp.arange(s)
    k = _mla_keys(c, kpe, kpos, w_uk, gk)
    nb = s // Q_BLOCK
    qb = jnp.moveaxis(q.reshape(b, nb, Q_BLOCK, C_HEADS, C_QK), 1, 0)

    def block(args):
        i, qi = args
        qpos = i * Q_BLOCK + jnp.arange(Q_BLOCK)
        return _mla_core(qi, k, c, w_uv, qpos[:, None] >= kpos[None, :])

    o = lax.map(block, (jnp.arange(nb), qb))
    return jnp.moveaxis(o, 0, 1).reshape(b, s, C_HEADS * C_V)


def _mla_sample(q, c_new, kpe_new, pool_c, pool_kpe, li, page_table, w_uk, w_uv, gk):
    t = q.shape[1]
    past = page_table.shape[1] * PAGE_SIZE
    kpos = jnp.arange(past + t)
    mask = (past + jnp.arange(t))[:, None] >= kpos[None, :]

    def one(args):
        qb, cb, pb, pages = args
        c_all = jnp.concatenate([pool_c[li, pages].reshape(past, C_LATENT), cb], axis=0)[None]
        p_all = jnp.concatenate([pool_kpe[li, pages].reshape(past, C_ROPE), pb], axis=0)[None]
        k = _mla_keys(c_all, p_all, kpos, w_uk, gk)
        return _mla_core(qb[None], k, c_all, w_uv, mask)[0]

    return lax.map(one, (q, c_new, kpe_new, page_table))


def _chunk_gate(u, v, w_s, b_s):
    b, nc, l, _ = u.shape
    w = jnp.tril(w_s[:, :l, :l])
    vg = v.reshape(b, nc, l, D_GROUPS, D_GROUP_DIM)
    mix = jnp.einsum('gij,bcjgd->bcigd', w, vg) + jnp.swapaxes(b_s[:, :l], 0, 1)[:, :, None]
    return u * mix.reshape(b, nc, l, D_WIDTH)


def _mem_kv(mem, wk, wv, gk):
    b, m, _ = mem.shape
    k = _rms((mem @ wk).reshape(b, m, X_HEADS, X_DIM), gk)
    v = (mem @ wv).reshape(b, m, X_HEADS, X_DIM)
    return k, v


def _cross(h, mk, mv, wq, gq, wo):
    b, t, _ = h.shape
    q = _rms((h @ wq).reshape(b, t, X_HEADS, X_DIM), gq)
    s = jnp.einsum('bthd,bmhd->bhtm', q, mk, preferred_element_type=jnp.float32) * (X_DIM ** -0.5)
    p = jax.nn.softmax(s, axis=-1).astype(mv.dtype)
    o = jnp.einsum('bhtm,bmhd->bthd', p, mv).reshape(b, t, X_WIDTH)
    return o @ wo


def _hmoe(h, wg, bg, we, be, w1, w3, w2):
    b, t, d = h.shape
    x = h.reshape(b * t, d)
    g_logits = jnp.einsum('nd,dg->ng', x, wg, preferred_element_type=jnp.float32) + bg.astype(jnp.float32)
    g_sel = jnp.argmax(g_logits, axis=-1)
    g_prob = jnp.max(jax.nn.softmax(g_logits, axis=-1), axis=-1, keepdims=True)
    g_onehot = jax.nn.one_hot(g_sel, N_GROUPS, dtype=jnp.float32)
    e_logits = jnp.einsum('nd,dge->nge', x, we, preferred_element_type=jnp.float32) + be.astype(jnp.float32)
    e_logits = jnp.einsum('nge,ng->ne', e_logits, g_onehot)
    top_l, top_i = lax.top_k(e_logits, TOP_K)
    top_w = jax.nn.softmax(top_l, axis=-1) * g_prob
    e_w = jnp.einsum('nk,nke->ne', top_w, jax.nn.one_hot(top_i, EXPERTS_PER_GROUP, dtype=jnp.float32))
    gate = (g_onehot[:, :, None] * e_w[:, None, :]).reshape(b * t, N_EXPERTS)
    hid = jax.nn.silu(jnp.einsum('nd,xdf->nxf', x, w1)) * jnp.einsum('nd,xdf->nxf', x, w3)
    y = jnp.einsum('nxf,xfd->nd', hid * gate.astype(hid.dtype)[:, :, None], w2)
    return y.reshape(b, t, d)


def setup_inputs(seed: int = 0) -> dict:
    key = jax.random.key(seed)
    keys = jax.random.split(key, 64)
    kit = iter(range(64))
    f32 = jnp.float32

    def nrm(shape, scale=1.0):
        return jax.random.normal(keys[next(kit)], shape, f32) * scale

    def gain(shape):
        return 1.0 + nrm(shape, 0.02)

    n_pages = PAST_LEN // PAGE_SIZE
    n_used = DEC_BATCH * n_pages
    n_pool = n_used + max(1, n_used // 4)
    page_table = jax.random.permutation(keys[next(kit)], n_pool)[:n_used].reshape(DEC_BATCH, n_pages).astype(jnp.int32)
    dm = D_MODEL ** -0.5
    return {
        'x_prompt': nrm((BATCH, SEQ, D_MODEL)),
        'x_sample': nrm((DEC_BATCH, DEC_SEQ, D_MODEL)),
        'cache_a_k': nrm((N_EVEN, n_pool, PAGE_SIZE, A_KV_HEADS, 2, A_DIM)),
        'cache_a_v': nrm((N_EVEN, n_pool, PAGE_SIZE, A_KV_HEADS, 2 * A_DIM)),
        'state_pool_hist': nrm((N_EVEN, DEC_BATCH, B_HIST, B_WIDTH)),
        'cache_c_latent': nrm((N_ODD, n_pool, PAGE_SIZE, C_LATENT)),
        'cache_c_kpe': nrm((N_ODD, n_pool, PAGE_SIZE, C_ROPE)),
        'cache_mem_k': nrm((DEPTH, DEC_BATCH, MEM_LEN, X_HEADS, X_DIM)),
        'cache_mem_v': nrm((DEPTH, DEC_BATCH, MEM_LEN, X_HEADS, X_DIM)),
        'page_table': page_table,
        'mem_prompt': nrm((BATCH, MEM_LEN, D_MODEL)),
        'norm_mix': gain((DEPTH, D_MODEL)),
        'norm_mem': gain((DEPTH, D_MODEL)),
        'norm_ffn': gain((DEPTH, D_MODEL)),
        'l0_w_in': nrm((N_EVEN, D_MODEL, L0_COLS), dm),
        'l0_w_out': nrm((N_EVEN, L0_OUT, D_MODEL), L0_OUT ** -0.5),
        'a_gq': gain((N_EVEN, 2, A_DIM)),
        'a_gk': gain((N_EVEN, 2, A_DIM)),
        'a_lam': nrm((N_EVEN, 4, A_DIM), 0.1),
        'a_g_out': gain((N_EVEN, 2 * A_DIM)),
        'b_w': nrm((N_EVEN, B_GROUPS, B_GROUP_DIM, B_GROUP_DIM), B_GROUP_DIM ** -0.5),
        'b_scale': gain((N_EVEN, B_WIDTH)),
        'l1_w_in': nrm((N_ODD, D_MODEL, L1_COLS), dm),
        'l1_w_out': nrm((N_ODD, L1_OUT, D_MODEL), L1_OUT ** -0.5),
        'c_g_latent': gain((N_ODD, C_LATENT)),
        'c_w_uk': nrm((N_ODD, C_LATENT, C_HEADS, C_NOPE), C_LATENT ** -0.5),
        'c_w_uv': nrm((N_ODD, C_LATENT, C_HEADS, C_V), C_LATENT ** -0.5),
        'c_gq': gain((N_ODD, C_QK)),
        'c_gk': gain((N_ODD, C_QK)),
        'd_g_v': gain((N_ODD, D_GROUPS, D_GROUP_DIM)),
        'd_w_s': nrm((N_ODD, D_GROUPS, D_CHUNK, D_CHUNK), D_CHUNK ** -0.5),
        'd_b_s': 1.0 + nrm((N_ODD, D_GROUPS, D_CHUNK), 0.02),
        'x_wq': nrm((DEPTH, D_MODEL, X_WIDTH), dm),
        'x_wk': nrm((DEPTH, D_MODEL, X_WIDTH), dm),
        'x_wv': nrm((DEPTH, D_MODEL, X_WIDTH), dm),
        'x_gq': gain((DEPTH, X_DIM)),
        'x_gk': gain((DEPTH, X_DIM)),
        'x_wo': nrm((DEPTH, X_WIDTH, D_MODEL), X_WIDTH ** -0.5),
        'moe_wg': nrm((DEPTH, D_MODEL, N_GROUPS), dm),
        'moe_bg': nrm((DEPTH, N_GROUPS), 0.01),
        'moe_we': nrm((DEPTH, D_MODEL, N_GROUPS, EXPERTS_PER_GROUP), dm),
        'moe_be': nrm((DEPTH, N_GROUPS, EXPERTS_PER_GROUP), 0.01),
        'moe_w1': nrm((DEPTH, N_EXPERTS, D_MODEL, D_EXPERT), dm),
        'moe_w3': nrm((DEPTH, N_EXPERTS, D_MODEL, D_EXPERT), dm),
        'moe_w2': nrm((DEPTH, N_EXPERTS, D_EXPERT, D_MODEL), D_EXPERT ** -0.5),
    }


def reference(x_prompt, x_sample, cache_a_k, cache_a_v, state_pool_hist, cache_c_latent, cache_c_kpe,
              cache_mem_k, cache_mem_v, page_table, mem_prompt, norm_mix, norm_mem, norm_ffn,
              l0_w_in, l0_w_out, a_gq, a_gk, a_lam, a_g_out, b_w, b_scale,
              l1_w_in, l1_w_out, c_g_latent, c_w_uk, c_w_uv, c_gq, c_gk, d_g_v, d_w_s, d_b_s,
              x_wq, x_wk, x_wv, x_gq, x_gk, x_wo,
              moe_wg, moe_bg, moe_we, moe_be, moe_w1, moe_w3, moe_w2):
    b, s, _ = x_prompt.shape
    db, t, _ = x_sample.shape
    pos_p = jnp.arange(s)
    pos_s = PAST_LEN + jnp.arange(t)
    xp, xs = x_prompt, x_sample
    ak_p, av_p, pl_p, cl_p, ck_p, mk_p, mv_p = [], [], [], [], [], [], []
    ak_s, av_s, pl_s, cl_s, ck_s, dv_s = [], [], [], [], [], []
    for layer in range(DEPTH):
        hp = _rms(xp, norm_mix[layer])
        hs = _rms(xs, norm_mix[layer])
        if layer % 2 == 0:
            e = layer // 2
            lam_init = 0.8 - 0.6 * math.exp(-0.3 * layer)
            lam = _diff_lambda(a_lam[e], lam_init)
            qp, kp, vp, bp = _split_even(hp @ l0_w_in[e], a_gq[e], a_gk[e])
            qs, ks, vs, bs = _split_even(hs @ l0_w_in[e], a_gq[e], a_gk[e])
            oa_p = _diff_prompt(qp, kp, vp, lam, a_g_out[e], lam_init)
            oa_s = _diff_sample(qs, ks, vs, cache_a_k, cache_a_v, e, page_table, lam, a_g_out[e], lam_init)
            ext_p = jnp.concatenate([jnp.zeros((b, B_HIST, B_WIDTH), bp.dtype), bp], axis=1)
            ext_s = jnp.concatenate([state_pool_hist[e], bs], axis=1)
            ob_p = _multi_pool(ext_p, 0, b_w[e], b_scale[e])
            ob_s = _multi_pool(ext_s, PAST_LEN, b_w[e], b_scale[e])
            xp = xp + jnp.concatenate([oa_p, ob_p], axis=-1) @ l0_w_out[e]
            xs = xs + jnp.concatenate([oa_s, ob_s], axis=-1) @ l0_w_out[e]
            ak_p.append(kp)
            av_p.append(vp)
            pl_p.append(ext_p[:, -B_HIST:])
            ak_s.append(ks)
            av_s.append(vs)
            pl_s.append(ext_s[:, -B_HIST:])
        else:
            o = layer // 2
            qp, cp, kpe_p, up, gv_p = _split_odd(hp @ l1_w_in[o], c_g_latent[o], d_g_v[o])
            qs, cs, kpe_s, us, gv_s = _split_odd(hs @ l1_w_in[o], c_g_latent[o], d_g_v[o])
            oc_p = _mla_prompt(_mla_queries(qp, pos_p, c_gq[o]), cp, kpe_p, c_w_uk[o], c_w_uv[o], c_gk[o])
            oc_s = _mla_sample(_mla_queries(qs, pos_s, c_gq[o]), cs, kpe_s, cache_c_latent, cache_c_kpe, o,
                               page_table, c_w_uk[o], c_w_uv[o], c_gk[o])
            nc = s // D_CHUNK
            od_p = _chunk_gate(up.reshape(b, nc, D_CHUNK, D_WIDTH), gv_p.reshape(b, nc, D_CHUNK, D_WIDTH),
                               d_w_s[o], d_b_s[o]).reshape(b, s, D_WIDTH)
            od_s = _chunk_gate(us[:, None], gv_s[:, None], d_w_s[o], d_b_s[o])[:, 0]
            xp = xp + jnp.concatenate([oc_p, od_p], axis=-1) @ l1_w_out[o]
            xs = xs + jnp.concatenate([oc_s, od_s], axis=-1) @ l1_w_out[o]
            cl_p.append(cp)
            ck_p.append(kpe_p)
            cl_s.append(cs)
            ck_s.append(kpe_s)
            dv_s.append(gv_s)
        mkp, mvp = _mem_kv(mem_prompt, x_wk[layer], x_wv[layer], x_gk[layer])
        xp = xp + _cross(_rms(xp, norm_mem[layer]), mkp, mvp, x_wq[layer], x_gq[layer], x_wo[layer])
        xs = xs + _cross(_rms(xs, norm_mem[layer]), cache_mem_k[layer], cache_mem_v[layer],
                         x_wq[layer], x_gq[layer], x_wo[layer])
        mk_p.append(mkp)
        mv_p.append(mvp)
        xp = xp + _hmoe(_rms(xp, norm_ffn[layer]), moe_wg[layer], moe_bg[layer], moe_we[layer], moe_be[layer],
                        moe_w1[layer], moe_w3[layer], moe_w2[layer])
        xs = xs + _hmoe(_rms(xs, norm_ffn[layer]), moe_wg[layer], moe_bg[layer], moe_we[layer], moe_be[layer],
                        moe_w1[layer], moe_w3[layer], moe_w2[layer])
    y_prompt = xp
    y_sample = xs
    a_k_prompt = jnp.stack(ak_p)
    a_v_prompt = jnp.stack(av_p)
    pool_prompt = jnp.stack(pl_p)
    c_lat_prompt = jnp.stack(cl_p)
    c_kpe_prompt = jnp.stack(ck_p)
    mem_k_prompt = jnp.stack(mk_p)
    mem_v_prompt = jnp.stack(mv_p)
    a_k_sample = jnp.stack(ak_s)
    a_v_sample = jnp.stack(av_s)
    pool_sample = jnp.stack(pl_s)
    c_lat_sample = jnp.stack(cl_s)
    c_kpe_sample = jnp.stack(ck_s)
    d_v_sample = jnp.stack(dv_s)
    return (y_prompt, y_sample, a_k_prompt, a_v_prompt, pool_prompt, c_lat_prompt, c_kpe_prompt, mem_k_prompt, mem_v_prompt, a_k_sample, a_v_sample, pool_sample, c_lat_sample, c_kpe_sample, d_v_sample)
```

```python
import functools
import math

import jax
import jax.numpy as jnp
import numpy as np
from jax import lax
from jax.experimental import pallas as pl
from jax.experimental.pallas import tpu as pltpu

F32 = jnp.float32
BF16 = jnp.bfloat16
EPS = 1e-6
NEG_INF = -1e30
ROPE_THETA = 10000.0

LANES = 128
VMEM_LIMIT = 56 * 1024 * 1024

A_DIM = 64
A_KV_HEADS = 2
B_WINDOWS = (2, 4, 8, 16)
B_HIST = max(B_WINDOWS) - 1
C_NOPE = 128
C_ROPE = 64
C_QK = C_NOPE + C_ROPE
C_HEAD_PAD = 256
D_GROUPS = 4
D_CHUNK = 128
X_DIM = 128
TOP_K = 2
MOE_TILE = 256


def _params(*sem):
    return pltpu.CompilerParams(dimension_semantics=sem, vmem_limit_bytes=VMEM_LIMIT)


def _block_ones(n, seg):
    i = np.arange(n)
    return jnp.asarray((i[:, None] // seg) == (i[None, :] // seg), dtype=BF16)


def _rep_lanes(x, n):
    if n == LANES:
        return x
    if n < LANES:
        return x[:, :n]
    return jnp.concatenate([x] * (n // LANES), axis=1)


def _nt_dot(a, b):
    return lax.dot_general(a, b, (((1,), (1,)), ((), ())), preferred_element_type=F32)


def _mm_body(*refs, n_lhs, n_first, has_gain, n_aux, epilogue):
    per = 2 if n_first else 1
    xs = refs[:per * n_lhs]
    p = per * n_lhs
    gain = None
    if has_gain:
        gain = refs[p]
        p += 1
    ws = refs[p:p + n_lhs]
    p += n_lhs
    aux = refs[p:p + n_aux]
    outs = refs[p + n_aux:]
    acc = None
    for k, w_ref in enumerate(ws):
        if n_first:
            x = jnp.where(pl.program_id(0) < n_first, xs[2 * k][...].astype(BF16), xs[2 * k + 1][...].astype(BF16))
        else:
            x = xs[k][...]
        if gain is not None:
            xf = x.astype(F32)
            x = xf * lax.rsqrt(jnp.mean(xf * xf, axis=-1, keepdims=True) + EPS) * gain[...]
        y = jnp.dot(x.astype(BF16), w_ref[...], preferred_element_type=F32)
        acc = y if acc is None else acc + y
    epilogue(acc, aux, outs)


def _matmul(xs, ws, epilogue, outs, *, name, gain=None, aux=(), tm=512, tn=256, row0=0, rows=None):
    paired = isinstance(xs[0], tuple)
    n = ws[0].shape[1]
    n_first = 0
    if paired:
        assert row0 == 0 and rows is None and gain is None
        rows_first = xs[0][0].shape[0]
        rows = rows_first + xs[0][1].shape[0]
        tm = min(tm, rows_first, xs[0][1].shape[0])
        assert rows_first % tm == 0
        n_first = rows_first // tm
    else:
        rows = xs[0].shape[0] - row0 if rows is None else rows
        tm = min(tm, rows)
    assert rows % tm == 0 and row0 % tm == 0 and n % tn == 0
    r0 = row0 // tm
    n_j = n // tn
    grid = (rows // tm, n_j)
    in_specs, args = [], []
    for x in xs:
        if paired:
            in_specs += [pl.BlockSpec((tm, x[0].shape[1]), lambda i, j: (jnp.minimum(i, n_first - 1), 0)),
                         pl.BlockSpec((tm, x[1].shape[1]), lambda i, j: (jnp.maximum(i - n_first, 0), 0))]
            args += [x[0], x[1]]
        else:
            in_specs.append(pl.BlockSpec((tm, x.shape[1]), lambda i, j: (i + r0, 0)))
            args.append(x)
    if gain is not None:
        in_specs.append(pl.BlockSpec((1, gain.shape[1]), lambda i, j: (0, 0)))
        args.append(gain)
    in_specs += [pl.BlockSpec((w.shape[0], tn), lambda i, j: (0, j)) for w in ws]
    args += list(ws)
    for a, kind in aux:
        if kind == "const":
            spec = pl.BlockSpec(a.shape, lambda i, j: (0, 0))
        elif kind == "row":
            spec = pl.BlockSpec((tm, a.shape[1]), lambda i, j: (i + r0, 0))
        elif kind == "col":
            spec = pl.BlockSpec((1, a.shape[1] // n_j), lambda i, j: (0, j))
        else:
            spec = pl.BlockSpec((tm, tn), lambda i, j: (i + r0, j))
        in_specs.append(spec)
        args.append(a)
    out_shape = [jax.ShapeDtypeStruct((rows, c), dt) for (c, dt, _) in outs]
    out_specs = [pl.BlockSpec((tm, t), lambda i, j: (i, j)) for (_, _, t) in outs]
    body = functools.partial(_mm_body, n_lhs=len(xs), n_first=n_first, has_gain=gain is not None,
                             n_aux=len(aux), epilogue=epilogue)
    return pl.pallas_call(body, grid=grid, in_specs=in_specs, out_specs=out_specs, out_shape=out_shape,
                          compiler_params=_params("parallel", "arbitrary"), name=name)(*args)


def _ep_store(acc, aux, outs):
    outs[0][...] = acc.astype(outs[0].dtype)


def _ep_residual(acc, aux, outs):
    outs[0][...] = (aux[0][...] + acc).astype(outs[0].dtype)


def _seg_rms(y, ones_ref, gain_ref, seg):
    ssq = jnp.dot((y * y).astype(BF16), ones_ref[...], preferred_element_type=F32)
    return y * lax.rsqrt(ssq * (1.0 / seg) + EPS) * gain_ref[...]


def _ep_segnorm(seg, scale):
    def ep(acc, aux, outs):
        y = _seg_rms(acc, aux[0], aux[1], seg)
        if scale != 1.0:
            y = y * scale
        outs[0][...] = y.astype(outs[0].dtype)
    return ep


def _ep_diff_q(acc, aux, outs):
    y = _seg_rms(acc, aux[0], aux[1], A_DIM) * (A_DIM ** -0.5)
    lane = lax.broadcasted_iota(jnp.int32, (1, LANES), 1)
    lo = (lane < A_DIM).astype(F32)
    hi = 1.0 - lo
    o = outs[0]
    for h in range(2):
        blk = y[:, h * LANES:(h + 1) * LANES]
        o[:, (2 * h) * LANES:(2 * h + 1) * LANES] = (blk * lo).astype(o.dtype)
        o[:, (2 * h + 1) * LANES:(2 * h + 2) * LANES] = (blk * hi).astype(o.dtype)


def _ep_diff_kvb(acc, aux, outs):
    j = pl.program_id(1)

    @pl.when(j == 0)
    def _():
        outs[0][...] = _seg_rms(acc, aux[0], aux[1], A_DIM)

    @pl.when(j > 0)
    def _():
        outs[0][...] = acc


def _rope_half(b, cos, sin):
    lane = lax.broadcasted_iota(jnp.int32, b.shape, 1)
    half = C_ROPE // 2
    partner = jnp.where(lane < half, pltpu.roll(b, LANES - half, 1), pltpu.roll(b, half, 1))
    return b * cos + partner * sin


def _ep_head_rope(scale, with_kpe):
    def ep(acc, aux, outs):
        if with_kpe:
            g_ref, cos_ref, sin_ref, kpe_ref = aux
            y = jnp.concatenate([acc[:, :LANES], kpe_ref[...]], axis=1)
        else:
            g_ref, cos_ref, sin_ref = aux
            y = acc
        ms = jnp.sum(y * y, axis=-1, keepdims=True) * (1.0 / C_QK)
        yn = y * lax.rsqrt(ms + EPS) * g_ref[...]
        br = _rope_half(yn[:, LANES:], cos_ref[...], sin_ref[...])
        o = jnp.concatenate([yn[:, :LANES], br], axis=1)
        if scale != 1.0:
            o = o * scale
        outs[0][...] = o.astype(outs[0].dtype)
    return ep


def _ep_latent(c_latent):
    def ep(acc, aux, outs):
        c = acc[:, :c_latent]
        ms = jnp.mean(c * c, axis=-1, keepdims=True)
        outs[0][...] = c * lax.rsqrt(ms + EPS) * aux[0][...]
        outs[1][...] = acc[:, c_latent:]
    return ep


def _gelu(x):
    return 0.5 * x * (1.0 + jnp.tanh(math.sqrt(2.0 / math.pi) * (x + 0.044715 * (x * x * x))))


def _ep_gelu(acc, aux, outs):
    outs[0][...] = _gelu(acc)


def _ep_gelu_rms(acc, aux, outs):
    y = _gelu(acc)
    ms = jnp.mean(y * y, axis=-1, keepdims=True)
    outs[0][...] = y * lax.rsqrt(ms + EPS) * aux[0][...]


def _diff_lambda(lam_ref, lam_init):
    lp = lam_ref[...]
    a = jnp.sum(lp[0:1] * lp[1:2], axis=1, keepdims=True)
    b = jnp.sum(lp[2:3] * lp[3:4], axis=1, keepdims=True)
    return jnp.exp(a) - jnp.exp(b) + lam_init


def _diff_combine(o1, o2, lam, gout_ref, lam_init):
    d = o1 - lam * o2
    ms = jnp.mean(d * d, axis=-1, keepdims=True)
    return d * lax.rsqrt(ms + EPS) * gout_ref[...] * (1.0 - lam_init)


def _softmax_step(s, vt, m_s, l_s, acc_s, h):
    lk = s.shape[1]
    m_prev = m_s[h]
    m_new = jnp.maximum(m_prev, jnp.max(s, axis=1, keepdims=True))
    alpha = jnp.exp(m_prev - m_new)
    p = jnp.exp(s - _rep_lanes(m_new, lk))
    l_s[h] = alpha * l_s[h] + jnp.sum(p, axis=1, keepdims=True)
    pv = jnp.dot(p.astype(vt.dtype), vt, preferred_element_type=F32)
    acc_s[h] = _rep_lanes(alpha, pv.shape[1]) * acc_s[h] + pv
    m_s[h] = m_new


def _flash_body(*refs, nhk, g, dq, dv, tq, tk, skv, causal, diff, lam_init):
    if diff:
        q_ref, k_ref, v_ref, lam_ref, gout_ref, o_ref, m_s, l_s, acc_s = refs
    else:
        q_ref, k_ref, v_ref, o_ref, m_s, l_s, acc_s = refs
    qi = pl.program_id(2)
    m_s[...] = jnp.full(m_s.shape, NEG_INF, F32)
    l_s[...] = jnp.zeros(l_s.shape, F32)
    acc_s[...] = jnp.zeros(acc_s.shape, F32)

    def step(ki, masked):
        k0 = pl.multiple_of(ki * tk, tk)
        if masked:
            rows = qi * tq + lax.broadcasted_iota(jnp.int32, (tq, tk), 0)
            cols = k0 + lax.broadcasted_iota(jnp.int32, (tq, tk), 1)
            keep = rows >= cols
        for hk in range(nhk):
            kt = k_ref[pl.ds(k0, tk), hk * dq:(hk + 1) * dq].astype(BF16)
            vt = v_ref[pl.ds(k0, tk), hk * dv:(hk + 1) * dv].astype(BF16)
            for gi in range(g):
                h = hk * g + gi
                s = _nt_dot(q_ref[:, h * dq:(h + 1) * dq].astype(BF16), kt)
                if masked:
                    s = jnp.where(keep, s, NEG_INF)
                _softmax_step(s, vt, m_s, l_s, acc_s, h)

    def loop(lo, hi, masked):
        def body(ki, c):
            step(ki, masked)
            return c
        lax.fori_loop(lo, hi, body, 0)

    if causal:
        n_full = (qi * tq) // tk
        n_all = (qi * tq + tq + tk - 1) // tk
        loop(0, n_full, False)
        loop(n_full, n_all, True)
    elif skv == tk:
        step(0, False)
    else:
        loop(0, skv // tk, False)

    nh = nhk * g
    if diff:
        lam = _diff_lambda(lam_ref, lam_init)
        for j in range(nh // 2):
            o1 = acc_s[2 * j] / l_s[2 * j]
            o2 = acc_s[2 * j + 1] / l_s[2 * j + 1]
            o_ref[:, j * dv:(j + 1) * dv] = _diff_combine(o1, o2, lam, gout_ref, lam_init).astype(o_ref.dtype)
    else:
        for h in range(nh):
            o_ref[:, h * dv:(h + 1) * dv] = (acc_s[h] / l_s[h]).astype(o_ref.dtype)


def _flash(q, k, v, *, name, grid, q_map, k_map, v_map, o_map, nhk, g, dq, dv, tq, tk, skv, causal,
           out_rows, out_cols, out_dtype, diff=None):
    assert dv == LANES
    nh = nhk * g
    n_out_heads = nh // 2 if diff else nh
    in_specs = [pl.BlockSpec((tq, nh * dq), q_map),
                pl.BlockSpec((skv, nhk * dq), k_map),
                pl.BlockSpec((skv, nhk * dv), v_map)]
    args = [q, k, v]
    lam_init = 0.0
    if diff:
        lam_p, gout, lam_init = diff
        in_specs += [pl.BlockSpec(lam_p.shape, lambda b, h, i: (0, 0)),
                     pl.BlockSpec(gout.shape, lambda b, h, i: (0, 0))]
        args += [lam_p, gout]
    body = functools.partial(_flash_body, nhk=nhk, g=g, dq=dq, dv=dv, tq=tq, tk=tk, skv=skv, causal=causal,
                             diff=bool(diff), lam_init=lam_init)
    return pl.pallas_call(
        body, grid=grid, in_specs=in_specs,
        out_specs=pl.BlockSpec((tq, n_out_heads * dv), o_map),
        out_shape=jax.ShapeDtypeStruct((out_rows, out_cols), out_dtype),
        scratch_shapes=[pltpu.VMEM((nh, tq, LANES), F32), pltpu.VMEM((nh, tq, LANES), F32),
                        pltpu.VMEM((nh, tq, dv), F32)],
        compiler_params=_params("parallel", "parallel", "arbitrary"), name=name)(*args)


def _page_copies(caches, bufs, pt_ref, layer, b, c, slot, sems, pp, page):
    out = []
    for p in range(pp):
        pg = pt_ref[b, c * pp + p]
        for cache, buf in zip(caches, bufs):
            out.append(pltpu.make_async_copy(cache.at[layer, pg], buf.at[slot, pl.ds(p * page, page)],
                                             sems.at[slot]))
    return out


def _paged_loop(caches, bufs, pt_ref, layer, sems, pp, page, n_chunks, consume):
    b = pl.program_id(0)
    nb = pl.num_programs(0)

    def start(bb, cc, slot):
        for cp in _page_copies(caches, bufs, pt_ref, layer, bb, cc, slot, sems, pp, page):
            cp.start()

    @pl.when(b == 0)
    def _():
        start(0, 0, 0)

    def chunk(c, carry):
        slot = (b * n_chunks + c) % 2
        last = c == n_chunks - 1

        @pl.when(jnp.logical_or(jnp.logical_not(last), b + 1 < nb))
        def _():
            start(jnp.where(last, b + 1, b), jnp.where(last, 0, c + 1), 1 - slot)

        for cp in _page_copies(caches, bufs, pt_ref, layer, b, c, slot, sems, pp, page):
            cp.wait()
        consume(slot, c)
        return carry

    lax.fori_loop(0, n_chunks, chunk, 0)


def _diff_sample_body(pt_ref, q_ref, ck_hbm, cv_hbm, knew_ref, vnew_ref, lam_ref, gout_ref, o_ref,
                      kbuf, vbuf, sems, m_s, l_s, acc_s, *, layer, pp, page, n_chunks, tl, t_new, lam_init):
    m_s[...] = jnp.full(m_s.shape, NEG_INF, F32)
    l_s[...] = jnp.zeros(l_s.shape, F32)
    acc_s[...] = jnp.zeros(acc_s.shape, F32)
    q = q_ref[0]

    def consume(slot, c):
        for t in range(pp * page // tl):
            s = _nt_dot(q, kbuf[slot, pl.ds(t * tl, tl), :].astype(BF16))
            _softmax_step(s, vbuf[slot, pl.ds(t * tl, tl), :].astype(BF16), m_s, l_s, acc_s, 0)

    _paged_loop((ck_hbm, cv_hbm), (kbuf, vbuf), pt_ref, layer, sems, pp, page, n_chunks, consume)

    rows = q.shape[0]
    s = _nt_dot(q.astype(F32), knew_ref[...])
    t_row = lax.broadcasted_iota(jnp.int32, (rows, t_new), 0) % t_new
    l_col = lax.broadcasted_iota(jnp.int32, (rows, t_new), 1)
    s = jnp.where(t_row >= l_col, s, NEG_INF)
    _softmax_step(s, vnew_ref[...], m_s, l_s, acc_s, 0)
    o = acc_s[0] / _rep_lanes(l_s[0], acc_s.shape[2])
    lam = _diff_lambda(lam_ref, lam_init)
    half = rows // A_KV_HEADS
    grp = half // 2 // t_new
    for kh in range(A_KV_HEADS):
        ok = o[kh * half:(kh + 1) * half, kh * LANES:(kh + 1) * LANES]
        d = _diff_combine(ok[:half // 2], ok[half // 2:], lam, gout_ref, lam_init)
        for gi in range(grp):
            c0 = (kh * grp + gi) * LANES
            o_ref[:, c0:c0 + LANES] = d[gi * t_new:(gi + 1) * t_new].astype(o_ref.dtype)


def _diff_sample(qbd, cache_k, cache_v, e, page_table, kvb, lam_p, gout, lam_init, *, mp, t_new, pp, tl=512):
    db, n_pages = page_table.shape
    page = cache_k.shape[2]
    kw = cache_k.shape[3]
    rows = qbd.shape[1]
    assert n_pages % pp == 0 and (pp * page) % tl == 0
    r0 = mp // t_new
    out_w = rows // 2 * LANES // t_new
    in_specs = [pl.BlockSpec((1, rows, kw), lambda b, pt: (b, 0, 0)),
                pl.BlockSpec(memory_space=pl.ANY), pl.BlockSpec(memory_space=pl.ANY),
                pl.BlockSpec((t_new, kw), lambda b, pt: (r0 + b, 0)),
                pl.BlockSpec((t_new, kw), lambda b, pt: (r0 + b, 1)),
                pl.BlockSpec(lam_p.shape, lambda b, pt: (0, 0)),
                pl.BlockSpec(gout.shape, lambda b, pt: (0, 0))]
    gs = pltpu.PrefetchScalarGridSpec(
        num_scalar_prefetch=1, grid=(db,), in_specs=in_specs,
        out_specs=pl.BlockSpec((t_new, out_w), lambda b, pt: (b, 0)),
        scratch_shapes=[pltpu.VMEM((2, pp * page, kw), F32), pltpu.VMEM((2, pp * page, kw), F32),
                        pltpu.SemaphoreType.DMA((2,)),
                        pltpu.VMEM((1, rows, LANES), F32), pltpu.VMEM((1, rows, LANES), F32),
                        pltpu.VMEM((1, rows, kw), F32)])
    body = functools.partial(_diff_sample_body, layer=e, pp=pp, page=page, n_chunks=n_pages // pp, tl=tl,
                             t_new=t_new, lam_init=lam_init)
    return pl.pallas_call(body, grid_spec=gs, out_shape=jax.ShapeDtypeStruct((db * t_new, out_w), F32),
                          compiler_params=_params("arbitrary"), name="diff_sample")(
                              page_table, qbd, cache_k, cache_v, kvb, kvb, lam_p, gout)


def _pool_finish(d, g, w_ref, scale_ref, gd):
    y = jnp.dot(d.astype(BF16), w_ref[g], preferred_element_type=F32)
    return y * scale_ref[:, g * gd:(g + 1) * gd]


def _pool_prompt_body(*refs, tm, gd, n_grp):
    cur = refs[:n_grp]
    prev = refs[n_grp:2 * n_grp]
    w_ref, scale_ref, o_ref, ext = refs[2 * n_grp:]
    i = pl.program_id(1)
    hist = 16
    pos = i * tm + lax.broadcasted_iota(jnp.int32, (tm, 1), 0)
    for g, w in enumerate(B_WINDOWS):
        x = cur[g][...]
        ext[pl.ds(hist, tm), :] = x
        ext[pl.ds(0, hist), :] = jnp.where(i > 0, prev[g][...], 0.0)
        s = x
        for k in range(1, w):
            s = s + ext[pl.ds(hist - k, tm), :]
        cnt = jnp.minimum(pos + 1, w).astype(F32)
        d = s / cnt - x
        o_ref[:, g * gd:(g + 1) * gd] = _pool_finish(d, g, w_ref, scale_ref, gd).astype(o_ref.dtype)


def _pool_prompt(kvb, col0, w_lin, scale, *, batch, seq, rows_out, tm=512):
    n_grp, gd, _ = w_lin.shape
    nt = seq // tm
    c0 = col0 // gd
    hist = 16
    in_specs = [pl.BlockSpec((tm, gd), functools.partial(lambda b, i, g: (b * nt + i, c0 + g), g=g))
                for g in range(n_grp)]
    in_specs += [pl.BlockSpec((hist, gd), functools.partial(
        lambda b, i, g: (jnp.maximum((b * nt + i) * (tm // hist) - 1, 0), c0 + g), g=g)) for g in range(n_grp)]
    in_specs += [pl.BlockSpec(w_lin.shape, lambda b, i: (0, 0, 0)),
                 pl.BlockSpec(scale.shape, lambda b, i: (0, 0))]
    body = functools.partial(_pool_prompt_body, tm=tm, gd=gd, n_grp=n_grp)
    return pl.pallas_call(
        body, grid=(batch, nt), in_specs=in_specs,
        out_specs=pl.BlockSpec((tm, n_grp * gd), lambda b, i: (b * nt + i, 0)),
        out_shape=jax.ShapeDtypeStruct((rows_out, n_grp * gd), BF16),
        scratch_shapes=[pltpu.VMEM((tm + hist, gd), F32)],
        compiler_params=_params("parallel", "arbitrary"), name="pool_prompt")(
            *([kvb] * (2 * n_grp)), w_lin, scale)


def _pool_sample_body(ext_ref, w_ref, scale_ref, o_ref, *, bb, t_new, gd, past):
    pos = past + lax.broadcasted_iota(jnp.int32, (1, t_new, 1), 1)
    for g, w in enumerate(B_WINDOWS):
        x = ext_ref[:, pl.ds(B_HIST, t_new), pl.ds(g * gd, gd)]
        s = x
        for k in range(1, w):
            s = s + ext_ref[:, pl.ds(B_HIST - k, t_new), pl.ds(g * gd, gd)]
        cnt = jnp.minimum(pos + 1, w).astype(F32)
        d = (s / cnt - x).reshape(bb * t_new, gd)
        o_ref[:, g * gd:(g + 1) * gd] = _pool_finish(d, g, w_ref, scale_ref, gd).astype(o_ref.dtype)


def _pool_sample(ext, w_lin, scale, *, past, bb=16):
    db, n_ext, width = ext.shape
    t_new = n_ext - B_HIST
    n_grp, gd, _ = w_lin.shape
    body = functools.partial(_pool_sample_body, bb=bb, t_new=t_new, gd=gd, past=past)
    return pl.pallas_call(
        body, grid=(db // bb,),
        in_specs=[pl.BlockSpec((bb, n_ext, width), lambda i: (i, 0, 0)),
                  pl.BlockSpec(w_lin.shape, lambda i: (0, 0, 0)),
                  pl.BlockSpec(scale.shape, lambda i: (0, 0))],
        out_specs=pl.BlockSpec((bb * t_new, width), lambda i: (i, 0)),
        out_shape=jax.ShapeDtypeStruct((db * t_new, width), BF16),
        compiler_params=_params("parallel"), name="pool_sample")(ext, w_lin, scale)


def _chunk_gate_body(u_ref, v_ref, w_ref, b_ref, o_ref, *, gd, n_grp):
    for g in range(n_grp):
        cs = slice(g * gd, (g + 1) * gd)
        mix = jnp.dot(w_ref[g], v_ref[:, cs].astype(BF16), preferred_element_type=F32) + b_ref[:, cs]
        o_ref[:, cs] = (u_ref[:, cs] * mix).astype(o_ref.dtype)


def _chunk_gate(u, v, wmix, bias, *, row0, rows, name):
    width = u.shape[1]
    n_grp, r, _ = wmix.shape
    gd = width // n_grp
    r0 = row0 // r
    body = functools.partial(_chunk_gate_body, gd=gd, n_grp=n_grp)
    return pl.pallas_call(
        body, grid=(rows // r,),
        in_specs=[pl.BlockSpec((r, width), lambda i: (r0 + i, 0)),
                  pl.BlockSpec((r, width), lambda i: (r0 + i, 0)),
                  pl.BlockSpec(wmix.shape, lambda i: (0, 0, 0)),
                  pl.BlockSpec(bias.shape, lambda i: (0, 0))],
        out_specs=pl.BlockSpec((r, width), lambda i: (i, 0)),
        out_shape=jax.ShapeDtypeStruct((rows, width), BF16),
        compiler_params=_params("parallel"), name=name)(u, v, wmix, bias)


def _absorb_body(q_ref, w_ref, gk_ref, qabs_ref, qrot_ref, *, tb, t_new):
    q = q_ref[...].astype(F32)
    qn = (q[:, :C_NOPE] * gk_ref[...]).astype(BF16)
    y = jnp.dot(qn, w_ref[0], preferred_element_type=F32)
    qabs_ref[...] = y.reshape(tb, 1, t_new, y.shape[1])
    pe = q[:, C_NOPE:]
    lane = lax.broadcasted_iota(jnp.int32, pe.shape, 1)
    half = C_ROPE // 2
    swapped = jnp.where(lane < half, pltpu.roll(pe, LANES - half, 1), -pltpu.roll(pe, half, 1))
    swapped = jnp.where(lane < C_ROPE, swapped, 0.0)
    qrot = pe + pltpu.roll(swapped, C_ROPE, 1)
    qrot_ref[...] = qrot.reshape(tb, 1, t_new, LANES)


def _absorb(qh, wuk_t, gk_nope, *, mp, db, t_new, tb=16):
    n_heads, _, c_latent = wuk_t.shape
    tm = tb * t_new
    r0 = mp // tm
    body = functools.partial(_absorb_body, tb=tb, t_new=t_new)
    shp = lambda w: jax.ShapeDtypeStruct((db, n_heads, t_new, w), F32)
    spec = lambda w: pl.BlockSpec((tb, 1, t_new, w), lambda i, h: (i, h, 0, 0))
    return pl.pallas_call(
        body, grid=(db // tb, n_heads),
        in_specs=[pl.BlockSpec((tm, C_HEAD_PAD), lambda i, h: (r0 + i, h)),
                  pl.BlockSpec((1, C_NOPE, c_latent), lambda i, h: (h, 0, 0)),
                  pl.BlockSpec((1, C_NOPE), lambda i, h: (0, 0))],
        out_specs=[spec(c_latent), spec(LANES)],
        out_shape=[shp(c_latent), shp(LANES)],
        compiler_params=_params("parallel", "arbitrary"), name="mla_absorb")(qh, wuk_t, gk_nope)


def _mla_sample_body(pt_ref, qabs_ref, qrot_ref, wuk_ref, gk2_ref, tab_ref, cc_hbm, ck_hbm, cnew_ref, knew_ref,
                     tabn_ref, o_ref, cbuf, kbuf, sems, lhs_s, m_s, l_s, acc_s, *,
                     layer, pp, page, n_chunks, tl, t_new, n_heads):
    b = pl.program_id(0)
    n_up = wuk_ref.shape[0]
    hd = n_up // n_heads

    @pl.when(b == 0)
    def _():
        lhs_s[pl.ds(0, n_up), :] = wuk_ref[...]

    lhs_s[pl.ds(n_up, n_heads * t_new), :] = qabs_ref[0].astype(BF16)
    m_s[...] = jnp.full(m_s.shape, NEG_INF, F32)
    l_s[...] = jnp.zeros(l_s.shape, F32)
    acc_s[...] = jnp.zeros(acc_s.shape, F32)
    qrot = qrot_ref[0]
    gk2 = gk2_ref[...]
    ones = jnp.ones((8, C_ROPE), F32)

    def tile(lhs, c, kpe, tab, mask, dt):
        lk = c.shape[0]
        cb = c.astype(dt)
        big = _nt_dot(lhs, cb)
        kn = big[:n_up]
        nsq = jnp.sum((kn * kn).reshape(n_heads, hd, lk), axis=1)
        ksq = _nt_dot(ones, kpe * kpe)[:1]
        r = lax.rsqrt((nsq + ksq) * (1.0 / C_QK) + EPS)
        krot = jnp.concatenate([kpe, kpe], axis=1) * gk2 * tab
        s_pe = _nt_dot(qrot.astype(dt), krot.astype(dt))
        r_rows = jnp.concatenate([jnp.broadcast_to(r[h:h + 1], (t_new, lk)) for h in range(n_heads)], axis=0)
        s = (big[n_up:] + s_pe) * r_rows
        if mask is not None:
            s = jnp.where(mask, s, NEG_INF)
        _softmax_step(s, cb, m_s, l_s, acc_s, 0)

    def consume(slot, c):
        lhs = lhs_s[...]
        for t in range(pp * page // tl):
            l0 = pl.multiple_of(c * (pp * page) + t * tl, tl)
            tile(lhs, cbuf[slot, pl.ds(t * tl, tl), :], kbuf[slot, pl.ds(t * tl, tl), :],
                 tab_ref[pl.ds(l0, tl), :], None, BF16)

    _paged_loop((cc_hbm, ck_hbm), (cbuf, kbuf), pt_ref, layer, sems, pp, page, n_chunks, consume)

    rows = n_heads * t_new
    t_row = lax.broadcasted_iota(jnp.int32, (rows, t_new), 0) % t_new
    l_col = lax.broadcasted_iota(jnp.int32, (rows, t_new), 1)
    tile(lhs_s[...].astype(F32), cnew_ref[...], knew_ref[:, :C_ROPE], tabn_ref[...], t_row >= l_col, F32)
    o_ref[0] = acc_s[0] / _rep_lanes(l_s[0], acc_s.shape[2])


def _mla_sample(qabs, qrot, wuk2, gk2, tab, cache_c, cache_kpe, o, page_table, c_all, kpe_all, tab_new, *,
                mp, t_new, pp, tl=256):
    db, n_pages = page_table.shape
    page, c_latent = cache_c.shape[2], cache_c.shape[3]
    rows = qabs.shape[1]
    n_heads = rows // t_new
    n_up = wuk2.shape[0]
    assert n_pages % pp == 0 and (pp * page) % tl == 0
    r0 = mp // t_new
    in_specs = [pl.BlockSpec((1, rows, c_latent), lambda b, pt: (b, 0, 0)),
                pl.BlockSpec((1, rows, LANES), lambda b, pt: (b, 0, 0)),
                pl.BlockSpec(wuk2.shape, lambda b, pt: (0, 0)),
                pl.BlockSpec(gk2.shape, lambda b, pt: (0, 0)),
                pl.BlockSpec(tab.shape, lambda b, pt: (0, 0)),
                pl.BlockSpec(memory_space=pl.ANY), pl.BlockSpec(memory_space=pl.ANY),
                pl.BlockSpec((t_new, c_latent), lambda b, pt: (r0 + b, 0)),
                pl.BlockSpec((t_new, LANES), lambda b, pt: (r0 + b, 0)),
                pl.BlockSpec(tab_new.shape, lambda b, pt: (0, 0))]
    gs = pltpu.PrefetchScalarGridSpec(
        num_scalar_prefetch=1, grid=(db,), in_specs=in_specs,
        out_specs=pl.BlockSpec((1, rows, c_latent), lambda b, pt: (b, 0, 0)),
        scratch_shapes=[pltpu.VMEM((2, pp * page, c_latent), F32), pltpu.VMEM((2, pp * page, C_ROPE), F32),
                        pltpu.SemaphoreType.DMA((2,)),
                        pltpu.VMEM((n_up + rows, c_latent), BF16),
                        pltpu.VMEM((1, rows, LANES), F32), pltpu.VMEM((1, rows, LANES), F32),
                        pltpu.VMEM((1, rows, c_latent), F32)])
    body = functools.partial(_mla_sample_body, layer=o, pp=pp, page=page, n_chunks=n_pages // pp, tl=tl,
                             t_new=t_new, n_heads=n_heads)
    return pl.pallas_call(body, grid_spec=gs, out_shape=jax.ShapeDtypeStruct((db, rows, c_latent), F32),
                          compiler_params=_params("arbitrary"), name="mla_sample")(
                              page_table, qabs, qrot, wuk2, gk2, tab, cache_c, cache_kpe, c_all, kpe_all, tab_new)


def _mla_up_body(ctx_ref, w_ref, o_ref, *, tb, t_new):
    c = ctx_ref[...].reshape(tb * t_new, ctx_ref.shape[3]).astype(BF16)
    o_ref[...] = jnp.dot(c, w_ref[0], preferred_element_type=F32).astype(o_ref.dtype)


def _mla_up(ctx4, wuv_h, *, tb=16):
    db, n_heads, t_new, c_latent = ctx4.shape
    dv = wuv_h.shape[2]
    body = functools.partial(_mla_up_body, tb=tb, t_new=t_new)
    return pl.pallas_call(
        body, grid=(db // tb, n_heads),
        in_specs=[pl.BlockSpec((tb, 1, t_new, c_latent), lambda i, h: (i, h, 0, 0)),
                  pl.BlockSpec((1, c_latent, dv), lambda i, h: (h, 0, 0))],
        out_specs=pl.BlockSpec((tb * t_new, dv), lambda i, h: (i, h)),
        out_shape=jax.ShapeDtypeStruct((db * t_new, n_heads * dv), BF16),
        compiler_params=_params("parallel", "arbitrary"), name="mla_up")(ctx4, wuv_h)


def _router_body(x_ref, g_ref, w_ref, b_ref, h_ref, r_ref, *, n_groups, per_group):
    xf = x_ref[...]
    h = xf * lax.rsqrt(jnp.mean(xf * xf, axis=-1, keepdims=True) + EPS) * g_ref[...]
    h_ref[...] = h
    hh = h.astype(BF16)
    hl = (h - hh.astype(F32)).astype(BF16)
    w = w_ref[...]
    y1 = jnp.dot(hh, w, preferred_element_type=F32)
    y2 = jnp.dot(hl, w[:, :LANES], preferred_element_type=F32)
    logits = y1[:, :LANES] + y1[:, LANES:] + y2 + b_ref[...]
    lane = lax.broadcasted_iota(jnp.int32, logits.shape, 1)
    big = jnp.int32(1 << 20)
    low = -3.0e38
    n_e = n_groups * per_group
    is_g = lane < n_groups
    gl = jnp.where(is_g, logits, low)
    gmax = jnp.max(gl, axis=1, keepdims=True)
    gsel = jnp.min(jnp.where(gl == gmax, lane, big), axis=1, keepdims=True)
    gsum = jnp.sum(jnp.where(is_g, jnp.exp(gl - gmax), 0.0), axis=1, keepdims=True)
    gprob = 1.0 / gsum
    e_lane = lane - n_groups
    e_grp = jnp.right_shift(e_lane, per_group.bit_length() - 1)
    in_grp = jnp.logical_and(jnp.logical_and(e_lane >= 0, e_lane < n_e), e_grp == gsel)
    el = jnp.where(in_grp, logits, low)
    t1 = jnp.max(el, axis=1, keepdims=True)
    i1 = jnp.min(jnp.where(el == t1, lane, big), axis=1, keepdims=True)
    el2 = jnp.where(lane == i1, low, el)
    t2 = jnp.max(el2, axis=1, keepdims=True)
    i2 = jnp.min(jnp.where(el2 == t2, lane, big), axis=1, keepdims=True)
    ex = jnp.exp(t2 - t1)
    w1 = gprob / (1.0 + ex)
    w2 = gprob * ex / (1.0 + ex)
    out = jnp.where(lane == 0, (i1 - n_groups).astype(F32), 0.0)
    out = jnp.where(lane == 1, (i2 - n_groups).astype(F32), out)
    out = jnp.where(lane == 2, w1, out)
    out = jnp.where(lane == 3, w2, out)
    r_ref[...] = out


def _router(x, gain, wcat, bias, *, n_groups, per_group, tm=512):
    rows, d = x.shape
    body = functools.partial(_router_body, n_groups=n_groups, per_group=per_group)
    return pl.pallas_call(
        body, grid=(rows // tm,),
        in_specs=[pl.BlockSpec((tm, d), lambda i: (i, 0)), pl.BlockSpec((1, d), lambda i: (0, 0)),
                  pl.BlockSpec(wcat.shape, lambda i: (0, 0)), pl.BlockSpec(bias.shape, lambda i: (0, 0))],
        out_specs=[pl.BlockSpec((tm, d), lambda i: (i, 0)), pl.BlockSpec((tm, LANES), lambda i: (i, 0))],
        out_shape=[jax.ShapeDtypeStruct((rows, d), F32), jax.ShapeDtypeStruct((rows, LANES), F32)],
        compiler_params=_params("parallel"), name="moe_router")(x, gain, wcat, bias)


def _gather_copy(src_hbm, idx_ref, base, r, dst, sem):
    return pltpu.make_async_copy(src_hbm.at[pl.ds(idx_ref[base + r], 1)], dst.at[pl.ds(r, 1)], sem)


def _start_gather(src_hbm, idx_ref, base, n, dst, sem):
    def body(r, c):
        _gather_copy(src_hbm, idx_ref, base, r, dst, sem).start()
        return c
    lax.fori_loop(0, n, body, 0, unroll=8)


def _wait_gather(src_hbm, idx_ref, base, n, dst, sem):
    def body(r, c):
        _gather_copy(src_hbm, idx_ref, base, r, dst, sem).wait()
        return c
    lax.fori_loop(0, n, body, 0, unroll=8)


def _experts_body(te_ref, nt_ref, tok_ref, h_hbm, w1_ref, w3_ref, w2_ref, o_ref, xbuf, sems, w1b, w3b, w2b, *, tm):
    t = pl.program_id(0)
    nt = nt_ref[0]
    slot = t % 2

    @pl.when(jnp.logical_and(t == 0, nt > 0))
    def _():
        _start_gather(h_hbm, tok_ref, 0, tm, xbuf.at[0], sems.at[0])

    @pl.when(t + 1 < nt)
    def _():
        _start_gather(h_hbm, tok_ref, (t + 1) * tm, tm, xbuf.at[1 - slot], sems.at[1 - slot])

    @pl.when(t < nt)
    def _():
        changed = jnp.logical_or(t == 0, te_ref[t] != te_ref[jnp.maximum(t - 1, 0)])

        @pl.when(changed)
        def _():
            w1b[...] = w1_ref[0].astype(BF16)
            w3b[...] = w3_ref[0].astype(BF16)
            w2b[...] = w2_ref[0].astype(BF16)

        _wait_gather(h_hbm, tok_ref, t * tm, tm, xbuf.at[slot], sems.at[slot])
        x = xbuf[slot].astype(BF16)
        a = jnp.dot(x, w1b[...], preferred_element_type=F32)
        b = jnp.dot(x, w3b[...], preferred_element_type=F32)
        hid = (a * jax.nn.sigmoid(a)) * b
        o_ref[...] = jnp.dot(hid.astype(BF16), w2b[...], preferred_element_type=F32)

    @pl.when(t >= nt)
    def _():
        o_ref[...] = jnp.zeros(o_ref.shape, o_ref.dtype)


def _experts(h, tile_e, n_tiles, slot_tok, w1, w3, w2, *, tm):
    n_slots = slot_tok.shape[0]
    t_max = n_slots // tm
    d = h.shape[1]
    f = w1.shape[2]
    gs = pltpu.PrefetchScalarGridSpec(
        num_scalar_prefetch=3, grid=(t_max,),
        in_specs=[pl.BlockSpec(memory_space=pl.ANY),
                  pl.BlockSpec((1, d, f), lambda t, te, nt, tok: (te[t], 0, 0)),
                  pl.BlockSpec((1, d, f), lambda t, te, nt, tok: (te[t], 0, 0)),
                  pl.BlockSpec((1, f, d), lambda t, te, nt, tok: (te[t], 0, 0))],
        out_specs=pl.BlockSpec((tm, d), lambda t, te, nt, tok: (t, 0)),
        scratch_shapes=[pltpu.VMEM((2, tm, d), F32), pltpu.SemaphoreType.DMA((2,)),
                        pltpu.VMEM((d, f), BF16), pltpu.VMEM((d, f), BF16), pltpu.VMEM((f, d), BF16)])
    body = functools.partial(_experts_body, tm=tm)
    return pl.pallas_call(body, grid_spec=gs, out_shape=jax.ShapeDtypeStruct((n_slots, d), F32),
                          compiler_params=_params("arbitrary"), name="moe_experts")(
                              tile_e, n_tiles, slot_tok, h, w1, w3, w2)


def _combine_body(slot_ref, x_ref, r_ref, y_hbm, o_ref, buf, sems, *, tm, top_k):
    i = pl.program_id(0)
    n = pl.num_programs(0)
    slot = i % 2
    nrow = tm * top_k

    @pl.when(i == 0)
    def _():
        _start_gather(y_hbm, slot_ref, 0, nrow, buf.at[0], sems.at[0])

    @pl.when(i + 1 < n)
    def _():
        _start_gather(y_hbm, slot_ref, (i + 1) * nrow, nrow, buf.at[1 - slot], sems.at[1 - slot])

    _wait_gather(y_hbm, slot_ref, i * nrow, nrow, buf.at[slot], sems.at[slot])
    acc = x_ref[...]
    r = r_ref[...]
    for k in range(top_k):
        acc = acc + r[:, top_k + k:top_k + k + 1] * buf[slot, pl.ds(k * tm, tm), :]
    o_ref[...] = acc


def _combine(x, route, ys, slots, *, tm=256):
    rows, d = x.shape
    gs = pltpu.PrefetchScalarGridSpec(
        num_scalar_prefetch=1, grid=(rows // tm,),
        in_specs=[pl.BlockSpec((tm, d), lambda i, s: (i, 0)), pl.BlockSpec((tm, LANES), lambda i, s: (i, 0)),
                  pl.BlockSpec(memory_space=pl.ANY)],
        out_specs=pl.BlockSpec((tm, d), lambda i, s: (i, 0)),
        scratch_shapes=[pltpu.VMEM((2, tm * TOP_K, d), F32), pltpu.SemaphoreType.DMA((2,))])
    body = functools.partial(_combine_body, tm=tm, top_k=TOP_K)
    return pl.pallas_call(body, grid_spec=gs, out_shape=jax.ShapeDtypeStruct((rows, d), F32),
                          compiler_params=_params("arbitrary"), name="moe_combine")(slots, x, route, ys)


def _moe_plan(route, n_experts, tm, comb_tm):
    rows = route.shape[0]
    ids = route[:, :TOP_K].astype(jnp.int32)
    flat = ids.reshape(-1)
    n_assign = flat.shape[0]
    order = jnp.argsort(flat, stable=True).astype(jnp.int32)
    counts = jnp.zeros((n_experts,), jnp.int32).at[flat].add(1)
    tiles = (counts + tm - 1) // tm
    tile_end = jnp.cumsum(tiles)
    pad_off = (tile_end - tiles) * tm
    sort_off = jnp.cumsum(counts) - counts
    e_sorted = flat[order]
    slot_sorted = pad_off[e_sorted] + jnp.arange(n_assign, dtype=jnp.int32) - sort_off[e_sorted]
    t_max = n_assign // tm + n_experts
    slot_tok = jnp.zeros((t_max * tm,), jnp.int32).at[slot_sorted].set(order // TOP_K)
    slot_of = jnp.zeros((n_assign,), jnp.int32).at[order].set(slot_sorted).reshape(rows, TOP_K)
    n_tiles = tile_end[-1]
    t_idx = jnp.arange(t_max, dtype=jnp.int32)
    tile_e = jnp.searchsorted(tile_end, jnp.minimum(t_idx, n_tiles - 1), side="right").astype(jnp.int32)
    tile_e = jnp.minimum(tile_e, n_experts - 1)
    slots = slot_of.reshape(rows // comb_tm, comb_tm, TOP_K).transpose(0, 2, 1).reshape(-1)
    return tile_e, n_tiles.reshape(1).astype(jnp.int32), slot_tok, slots


def _hmoe(x, gain, wg, bg, we, be, w1, w3, w2):
    d = x.shape[1]
    n_groups, per_group = we.shape[1], we.shape[2]
    n_experts = n_groups * per_group
    wr = jnp.concatenate([wg, we.reshape(d, n_experts)], axis=1)
    wr = jnp.pad(wr, ((0, 0), (0, LANES - wr.shape[1])))
    wr_hi = wr.astype(BF16)
    wr_lo = (wr - wr_hi.astype(F32)).astype(BF16)
    wcat = jnp.concatenate([wr_hi, wr_lo], axis=1)
    bias = jnp.pad(jnp.concatenate([bg, be.reshape(-1)]), (0, LANES - n_groups - n_experts)).reshape(1, LANES)
    h, route = _router(x, gain, wcat, bias.astype(F32), n_groups=n_groups, per_group=per_group)
    comb_tm = 256
    tile_e, n_tiles, slot_tok, slots = _moe_plan(route, n_experts, MOE_TILE, comb_tm)
    ys = _experts(h, tile_e, n_tiles, slot_tok, w1, w3, w2, tm=MOE_TILE)
    return _combine(x, route, ys, slots, tm=comb_tm)


def _rope_tables(pos, width, first_half_sign):
    half = C_ROPE // 2
    inv = ROPE_THETA ** (-np.arange(half, dtype=np.float64) * 2.0 / C_ROPE)
    ang = np.asarray(pos, np.float64)[:, None] * inv[None, :]
    cos = np.zeros((len(pos), width), np.float32)
    sin = np.zeros((len(pos), width), np.float32)
    cos[:, :half] = np.cos(ang)
    cos[:, half:C_ROPE] = np.cos(ang)
    sin[:, :half] = first_half_sign * np.sin(ang)
    sin[:, half:C_ROPE] = np.sin(ang)
    return cos, sin


def kernel(x_prompt, x_sample, cache_a_k, cache_a_v, state_pool_hist, cache_c_latent, cache_c_kpe, cache_mem_k, cache_mem_v, page_table, mem_prompt, norm_mix, norm_mem, norm_ffn, l0_w_in, l0_w_out, a_gq, a_gk, a_lam, a_g_out, b_w, b_scale, l1_w_in, l1_w_out, c_g_latent, c_w_uk, c_w_uv, c_gq, c_gk, d_g_v, d_w_s, d_b_s, x_wq, x_wk, x_wv, x_gq, x_gk, x_wo, moe_wg, moe_bg, moe_we, moe_be, moe_w1, moe_w3, moe_w2):
    batch, seq, d = x_prompt.shape
    db, t_new, _ = x_sample.shape
    depth = norm_mix.shape[0]
    n_pool, page = cache_a_k.shape[1], cache_a_k.shape[2]
    n_pages = page_table.shape[1]
    past = n_pages * page
    mem_len = mem_prompt.shape[1]
    mp, ms = batch * seq, db * t_new
    m = mp + ms
    a_heads = l0_w_out.shape[1] // 2 // (2 * A_DIM)
    a_group = a_heads // A_KV_HEADS
    a_q = a_heads * 2 * A_DIM
    a_k = A_KV_HEADS * 2 * A_DIM
    b_width = b_scale.shape[1]
    c_latent = c_w_uk.shape[1]
    c_heads = c_w_uk.shape[2]
    c_v = c_w_uv.shape[3]
    d_width = d_g_v.shape[1] * d_g_v.shape[2]
    x_heads = x_wq.shape[2] // X_DIM
    x_width = x_heads * X_DIM
    tq = 256
    nq = seq // tq
    pp = 16

    x = jnp.concatenate([x_prompt.reshape(mp, d), x_sample.reshape(ms, d)], axis=0)
    pos_all = np.concatenate([np.tile(np.arange(seq), batch), np.tile(past + np.arange(t_new), db)])
    cos_all, sin_all = _rope_tables(pos_all, LANES, -1.0)
    key_tab = jnp.asarray(np.concatenate(_rope_tables(np.arange(past), C_ROPE, 1.0), axis=1))
    new_tab = jnp.asarray(np.concatenate(_rope_tables(past + np.arange(t_new), C_ROPE, 1.0), axis=1))

    ak_l, av_l, pl_l, cl_l, ck_l, mk_l, mv_l, dv_l = [], [], [], [], [], [], [], []

    for layer in range(depth):
        gmix = norm_mix[layer].reshape(1, d)
        if layer % 2 == 0:
            e = layer // 2
            lam_init = 0.8 - 0.6 * math.exp(-0.3 * layer)
            w_in = l0_w_in[e]
            ones64 = _block_ones(256, A_DIM)
            gq_t = jnp.tile(a_gq[e].reshape(1, 2 * A_DIM), (1, 2))
            gk_t = jnp.tile(a_gk[e].reshape(1, 2 * A_DIM), (1, 2))
            (q_pad,) = _matmul([x], [w_in[:, :a_q].astype(BF16)], _ep_diff_q, [(2 * a_q, BF16, 512)],
                               name="l0_q", gain=gmix, aux=[(ones64, "const"), (gq_t, "const")])
            (kvb,) = _matmul([x], [w_in[:, a_q:].astype(BF16)], _ep_diff_kvb, [(2 * a_k + b_width, F32, 256)],
                             name="l0_kvb", gain=gmix, aux=[(ones64, "const"), (gk_t, "const")])
            lam_p = a_lam[e]
            gout = a_g_out[e].reshape(1, 2 * A_DIM)
            n_maps = a_group * 2
            oa = _flash(q_pad, kvb, kvb, name="diff_prompt", grid=(batch, A_KV_HEADS, nq),
                        q_map=lambda b, h, i: (b * nq + i, h), k_map=lambda b, h, i: (b, h),
                        v_map=lambda b, h, i: (b, A_KV_HEADS + h), o_map=lambda b, h, i: (b * nq + i, h),
                        nhk=1, g=n_maps, dq=LANES, dv=LANES, tq=tq, tk=256, skv=seq, causal=True,
                        out_rows=mp, out_cols=a_q, out_dtype=F32, diff=(lam_p, gout, lam_init))
            qs = q_pad[mp:].reshape(db, t_new, A_KV_HEADS, a_group, 2, LANES).transpose(0, 2, 4, 3, 1, 5)
            eye = jnp.eye(A_KV_HEADS, dtype=BF16)[None, :, None, None, None, :, None]
            qbd = (qs[:, :, :, :, :, None, :] * eye).reshape(db, A_KV_HEADS * n_maps * t_new, A_KV_HEADS * LANES)
            ck = cache_a_k.reshape(cache_a_k.shape[0], n_pool, page, a_k)
            cv = cache_a_v.reshape(cache_a_v.shape[0], n_pool, page, a_k)
            oa_s = _diff_sample(qbd, ck, cv, e, page_table, kvb, lam_p, gout, lam_init, mp=mp, t_new=t_new, pp=pp)
            b_wb = b_w[e].astype(BF16)
            b_sc = b_scale[e].reshape(1, b_width)
            ob = _pool_prompt(kvb, 2 * a_k, b_wb, b_sc, batch=batch, seq=seq, rows_out=mp)
            hb_s = kvb[mp:, 2 * a_k:].reshape(db, t_new, b_width)
            ext_s = jnp.concatenate([state_pool_hist[e], hb_s], axis=1)
            ob_s = _pool_sample(ext_s, b_wb, b_sc, past=past)
            w_out = l0_w_out[e].astype(BF16)
            (x,) = _matmul([(oa, oa_s), (ob, ob_s)], [w_out[:a_q], w_out[a_q:]], _ep_residual, [(d, F32, 512)],
                           name="l0_out",
                           aux=[(x, "tile")], tn=512)
            kvb_p = kvb[:mp].reshape(batch, seq, -1)
            kvb_s = kvb[mp:].reshape(db, t_new, -1)
            ak_l.append((kvb_p[..., :a_k].reshape(batch, seq, A_KV_HEADS, 2, A_DIM),
                         kvb_s[..., :a_k].reshape(db, t_new, A_KV_HEADS, 2, A_DIM)))
            av_l.append((kvb_p[..., a_k:2 * a_k].reshape(batch, seq, A_KV_HEADS, 2 * A_DIM),
                         kvb_s[..., a_k:2 * a_k].reshape(db, t_new, A_KV_HEADS, 2 * A_DIM)))
            pl_l.append((kvb_p[:, seq - B_HIST:, 2 * a_k:], ext_s[:, t_new:]))
        else:
            o = layer // 2
            w_in = l1_w_in[o]
            o1 = c_heads * C_QK
            o2 = o1 + c_latent
            o3 = o2 + C_ROPE
            o4 = o3 + d_width
            pad_h = C_HEAD_PAD - C_QK
            wq = jnp.pad(w_in[:, :o1].reshape(d, c_heads, C_QK), ((0, 0), (0, 0), (0, pad_h)))
            wq = wq.reshape(d, c_heads * C_HEAD_PAD).astype(BF16)
            gq_pad = jnp.pad(c_gq[o], (0, pad_h)).reshape(1, C_HEAD_PAD)
            gk_pad = jnp.pad(c_gk[o], (0, pad_h)).reshape(1, C_HEAD_PAD)
            rope_aux = lambda g: [(g, "const"), (jnp.asarray(cos_all), "row"), (jnp.asarray(sin_all), "row")]
            (qh,) = _matmul([x], [wq], _ep_head_rope(C_QK ** -0.5, False), [(c_heads * C_HEAD_PAD, BF16, 256)],
                            name="l1_q", gain=gmix, aux=rope_aux(gq_pad))
            wck = jnp.pad(w_in[:, o1:o3], ((0, 0), (0, LANES - C_ROPE))).astype(BF16)
            c_all, kpe_all = _matmul([x], [wck], _ep_latent(c_latent),
                                     [(c_latent, F32, c_latent), (LANES, F32, LANES)], name="l1_ckpe", gain=gmix,
                                     aux=[(c_g_latent[o].reshape(1, c_latent), "const")],
                                     tn=c_latent + LANES)
            (u,) = _matmul([x], [w_in[:, o3:o4].astype(BF16)], _ep_gelu, [(d_width, F32, 256)], name="l1_u",
                           gain=gmix)
            gvw = d_g_v.shape[2]
            (gv,) = _matmul([x], [w_in[:, o4:].astype(BF16)], _ep_gelu_rms, [(d_width, F32, gvw)], name="l1_v",
                            gain=gmix, tn=gvw,
                            aux=[(d_g_v[o].reshape(1, d_width), "col")])
            wuk_pad = jnp.pad(c_w_uk[o], ((0, 0), (0, 0), (0, C_HEAD_PAD - C_NOPE)))
            wuk_pad = wuk_pad.reshape(c_latent, c_heads * C_HEAD_PAD).astype(BF16)
            (kh,) = _matmul([c_all], [wuk_pad], _ep_head_rope(1.0, True), [(c_heads * C_HEAD_PAD, BF16, 256)],
                            name="l1_k", rows=mp,
                            aux=rope_aux(gk_pad) + [(kpe_all, "row")])
            (vh,) = _matmul([c_all], [c_w_uv[o].reshape(c_latent, c_heads * c_v).astype(BF16)], _ep_store,
                            [(c_heads * c_v, BF16, 256)], name="l1_vup", rows=mp)
            tq1 = 512
            nq1 = seq // tq1
            oc = _flash(qh, kh, vh, name="mla_prompt", grid=(batch, c_heads, nq1),
                        q_map=lambda b, h, i: (b * nq1 + i, h), k_map=lambda b, h, i: (b, h),
                        v_map=lambda b, h, i: (b, h), o_map=lambda b, h, i: (b * nq1 + i, h),
                        nhk=1, g=1, dq=C_HEAD_PAD, dv=c_v, tq=tq1, tk=256, skv=seq, causal=True,
                        out_rows=mp, out_cols=c_heads * c_v, out_dtype=BF16)
            wuk_t = jnp.transpose(c_w_uk[o], (1, 2, 0)).astype(BF16)
            qabs, qrot = _absorb(qh, wuk_t, c_gk[o][:C_NOPE].reshape(1, C_NOPE), mp=mp, db=db, t_new=t_new)
            rows_s = c_heads * t_new
            gk2 = jnp.tile(c_gk[o][C_NOPE:].reshape(1, C_ROPE), (1, 2))
            ctx = _mla_sample(qabs.reshape(db, rows_s, c_latent), qrot.reshape(db, rows_s, LANES),
                              wuk_t.reshape(c_heads * C_NOPE, c_latent), gk2, key_tab,
                              cache_c_latent, cache_c_kpe, o, page_table, c_all, kpe_all, new_tab,
                              mp=mp, t_new=t_new, pp=pp)
            wuv_h = jnp.transpose(c_w_uv[o], (1, 0, 2)).astype(BF16)
            oc_s = _mla_up(ctx.reshape(db, c_heads, t_new, c_latent), wuv_h)
            w_tri = jnp.tril(d_w_s[o])
            reps = D_CHUNK // t_new
            w_rep = jnp.einsum("ab,gij->gaibj", jnp.eye(reps, dtype=F32), w_tri[:, :t_new, :t_new])
            w_rep = w_rep.reshape(D_GROUPS, D_CHUNK, D_CHUNK)
            gd = d_width // D_GROUPS
            bias_p = jnp.repeat(d_b_s[o].T, gd, axis=1)
            bias_s = jnp.tile(bias_p[:t_new], (reps, 1))
            od = _chunk_gate(u, gv, w_tri.astype(BF16), bias_p, row0=0, rows=mp, name="gate_prompt")
            od_s = _chunk_gate(u, gv, w_rep.astype(BF16), bias_s, row0=mp, rows=ms, name="gate_sample")
            w_out = l1_w_out[o].astype(BF16)
            n_c = c_heads * c_v
            (x,) = _matmul([(oc, oc_s), (od, od_s)], [w_out[:n_c], w_out[n_c:]], _ep_residual, [(d, F32, 512)],
                           name="l1_out",
                           aux=[(x, "tile")], tn=512)
            cl_l.append((c_all[:mp].reshape(batch, seq, c_latent), c_all[mp:].reshape(db, t_new, c_latent)))
            ck_l.append((kpe_all[:mp, :C_ROPE].reshape(batch, seq, C_ROPE),
                         kpe_all[mp:, :C_ROPE].reshape(db, t_new, C_ROPE)))
            dv_l.append(gv[mp:].reshape(db, t_new, d_width))

        ones128 = _block_ones(x_width, X_DIM)
        gq_x = jnp.tile(x_gq[layer].reshape(1, X_DIM), (1, x_heads))
        gk_x = jnp.tile(x_gk[layer].reshape(1, X_DIM), (1, x_heads))
        (qx,) = _matmul([x], [x_wq[layer].astype(BF16)], _ep_segnorm(X_DIM, X_DIM ** -0.5), [(x_width, F32, x_width)],
                        name="x_q", gain=norm_mem[layer].reshape(1, d), tn=x_width,
                        aux=[(ones128, "const"), (gq_x, "const")])
        memf = mem_prompt.reshape(batch * mem_len, d)
        (mkp,) = _matmul([memf], [x_wk[layer].astype(BF16)], _ep_segnorm(X_DIM, 1.0), [(x_width, F32, x_width)],
                         name="x_k", tn=x_width, aux=[(ones128, "const"), (gk_x, "const")])
        (mvp,) = _matmul([memf], [x_wv[layer].astype(BF16)], _ep_store, [(x_width, F32, x_width)], name="x_v",
                         tn=x_width)
        ox = _flash(qx, mkp, mvp, name="cross_prompt", grid=(batch, 1, nq),
                    q_map=lambda b, h, i: (b * nq + i, 0), k_map=lambda b, h, i: (b, 0),
                    v_map=lambda b, h, i: (b, 0), o_map=lambda b, h, i: (b * nq + i, 0),
                    nhk=x_heads, g=1, dq=X_DIM, dv=X_DIM, tq=tq, tk=mem_len, skv=mem_len, causal=False,
                    out_rows=mp, out_cols=x_width, out_dtype=F32)
        cmk = cache_mem_k.reshape(depth * db * mem_len, x_width)
        cmv = cache_mem_v.reshape(depth * db * mem_len, x_width)
        r0s = mp // t_new
        ox_s = _flash(qx, cmk, cmv, name="cross_sample", grid=(db, 1, 1),
                      q_map=lambda b, h, i: (r0s + b, 0), k_map=lambda b, h, i: (layer * db + b, 0),
                      v_map=lambda b, h, i: (layer * db + b, 0), o_map=lambda b, h, i: (b, 0),
                      nhk=x_heads, g=1, dq=X_DIM, dv=X_DIM, tq=t_new, tk=mem_len, skv=mem_len, causal=False,
                      out_rows=ms, out_cols=x_width, out_dtype=F32)
        (x,) = _matmul([(ox, ox_s)], [x_wo[layer].astype(BF16)], _ep_residual, [(d, F32, 512)], name="x_out",
                       aux=[(x, "tile")], tn=512)
        mk_l.append(mkp.reshape(batch, mem_len, x_heads, X_DIM))
        mv_l.append(mvp.reshape(batch, mem_len, x_heads, X_DIM))

        x = _hmoe(x, norm_ffn[layer].reshape(1, d), moe_wg[layer], moe_bg[layer], moe_we[layer], moe_be[layer],
                  moe_w1[layer], moe_w3[layer], moe_w2[layer])

    stack = lambda items, k: jnp.stack([it[k] for it in items])
    return (x[:mp].reshape(batch, seq, d), x[mp:].reshape(db, t_new, d),
            stack(ak_l, 0), stack(av_l, 0), stack(pl_l, 0), stack(cl_l, 0), stack(ck_l, 0),
            jnp.stack(mk_l), jnp.stack(mv_l),
            stack(ak_l, 1), stack(av_l, 1), stack(pl_l, 1), stack(cl_l, 1), stack(ck_l, 1), jnp.stack(dv_l))
```

```python
import functools
import math

import jax
import jax.numpy as jnp
import numpy as np
from jax import lax
from jax.experimental import pallas as pl
from jax.experimental.pallas import tpu as pltpu

F32 = jnp.float32
BF16 = jnp.bfloat16
EPS = 1e-6
NEG_INF = -1e30
ROPE_THETA = 10000.0

LANES = 128
VMEM_LIMIT = 56 * 1024 * 1024

A_DIM = 64
A_KV_HEADS = 2
B_WINDOWS = (2, 4, 8, 16)
B_HIST = max(B_WINDOWS) - 1
C_NOPE = 128
C_ROPE = 64
C_QK = C_NOPE + C_ROPE
C_HEAD_PAD = 256
D_GROUPS = 4
D_CHUNK = 128
X_DIM = 128
TOP_K = 2
MOE_TILE = 256


def _params(*sem):
    return pltpu.CompilerParams(dimension_semantics=sem, vmem_limit_bytes=VMEM_LIMIT)


def _block_ones(n, seg):
    i = np.arange(n)
    return jnp.asarray((i[:, None] // seg) == (i[None, :] // seg), dtype=BF16)


def _rep_lanes(x, n):
    if n == LANES:
        return x
    if n < LANES:
        return x[:, :n]
    return jnp.concatenate([x] * (n // LANES), axis=1)


def _nt_dot(a, b):
    return lax.dot_general(a, b, (((1,), (1,)), ((), ())), preferred_element_type=F32)


def _mm_body(*refs, n_lhs, n_first, has_gain, n_aux, n_out, epilogue):
    per = 2 if n_first else 1
    xs = refs[:per * n_lhs]
    p = per * n_lhs
    gain = None
    if has_gain:
        gain = refs[p]
        p += 1
    ws = refs[p:p + n_lhs]
    p += n_lhs
    aux = refs[p:p + n_aux]
    p += n_aux
    outs = refs[p:p + n_out]
    lhs = refs[p + n_out:]

    @pl.when(pl.program_id(1) == 0)
    def _():
        for k in range(n_lhs):
            if n_first:
                x = jnp.where(pl.program_id(0) < n_first, xs[2 * k][...].astype(BF16),
                              xs[2 * k + 1][...].astype(BF16))
            else:
                x = xs[k][...]
            if gain is not None:
                xf = x.astype(F32)
                x = xf * lax.rsqrt(jnp.mean(xf * xf, axis=-1, keepdims=True) + EPS) * gain[...]
            lhs[k][...] = x.astype(BF16)

    acc = None
    for k, w_ref in enumerate(ws):
        y = jnp.dot(lhs[k][...], w_ref[...], preferred_element_type=F32)
        acc = y if acc is None else acc + y
    epilogue(acc, aux, outs)


def _matmul(xs, ws, epilogue, outs, *, name, gain=None, aux=(), tm=1024, tn=256, row0=0, rows=None):
    paired = isinstance(xs[0], tuple)
    n = ws[0].shape[1]
    n_first = 0
    if paired:
        assert row0 == 0 and rows is None and gain is None
        rows_first = xs[0][0].shape[0]
        rows = rows_first + xs[0][1].shape[0]
        tm = min(tm, rows_first, xs[0][1].shape[0])
        assert rows_first % tm == 0
        n_first = rows_first // tm
    else:
        rows = xs[0].shape[0] - row0 if rows is None else rows
        tm = min(tm, rows)
    assert rows % tm == 0 and row0 % tm == 0 and n % tn == 0
    r0 = row0 // tm
    n_j = n // tn
    grid = (rows // tm, n_j)
    in_specs, args = [], []
    for x in xs:
        if paired:
            in_specs += [pl.BlockSpec((tm, x[0].shape[1]), lambda i, j: (jnp.minimum(i, n_first - 1), 0)),
                         pl.BlockSpec((tm, x[1].shape[1]), lambda i, j: (jnp.maximum(i - n_first, 0), 0))]
            args += [x[0], x[1]]
        else:
            in_specs.append(pl.BlockSpec((tm, x.shape[1]), lambda i, j: (i + r0, 0)))
            args.append(x)
    if gain is not None:
        in_specs.append(pl.BlockSpec((1, gain.shape[1]), lambda i, j: (0, 0)))
        args.append(gain)
    in_specs += [pl.BlockSpec((w.shape[0], tn), lambda i, j: (0, j)) for w in ws]
    args += list(ws)
    for a, kind in aux:
        if kind == "const":
            spec = pl.BlockSpec(a.shape, lambda i, j: (0, 0))
        elif kind == "row":
            spec = pl.BlockSpec((tm, a.shape[1]), lambda i, j: (i + r0, 0))
        elif kind == "col":
            spec = pl.BlockSpec((1, a.shape[1] // n_j), lambda i, j: (0, j))
        else:
            spec = pl.BlockSpec((tm, tn), lambda i, j: (i + r0, j))
        in_specs.append(spec)
        args.append(a)
    out_shape = [jax.ShapeDtypeStruct((rows, c), dt) for (c, dt, _) in outs]
    out_specs = [pl.BlockSpec((tm, t), lambda i, j: (i, j)) for (_, _, t) in outs]
    body = functools.partial(_mm_body, n_lhs=len(xs), n_first=n_first, has_gain=gain is not None,
                             n_aux=len(aux), n_out=len(outs), epilogue=epilogue)
    scratch = [pltpu.VMEM((tm, w.shape[0]), BF16) for w in ws]
    return pl.pallas_call(body, grid=grid, in_specs=in_specs, out_specs=out_specs, out_shape=out_shape,
                          scratch_shapes=scratch,
                          compiler_params=_params("parallel", "arbitrary"), name=name)(*args)


def _ep_store(acc, aux, outs):
    outs[0][...] = acc.astype(outs[0].dtype)


def _ep_residual(acc, aux, outs):
    outs[0][...] = (aux[0][...] + acc).astype(outs[0].dtype)


def _seg_rms(y, ones_ref, gain_ref, seg):
    ssq = jnp.dot((y * y).astype(BF16), ones_ref[...], preferred_element_type=F32)
    return y * lax.rsqrt(ssq * (1.0 / seg) + EPS) * gain_ref[...]


def _ep_segnorm(seg, scale):
    def ep(acc, aux, outs):
        y = _seg_rms(acc, aux[0], aux[1], seg)
        if scale != 1.0:
            y = y * scale
        outs[0][...] = y.astype(outs[0].dtype)
    return ep


def _ep_diff_q(acc, aux, outs):
    y = _seg_rms(acc, aux[0], aux[1], A_DIM) * (A_DIM ** -0.5)
    lane = lax.broadcasted_iota(jnp.int32, (1, LANES), 1)
    lo = (lane < A_DIM).astype(F32)
    hi = 1.0 - lo
    o = outs[0]
    for h in range(2):
        blk = y[:, h * LANES:(h + 1) * LANES]
        o[:, (2 * h) * LANES:(2 * h + 1) * LANES] = (blk * lo).astype(o.dtype)
        o[:, (2 * h + 1) * LANES:(2 * h + 2) * LANES] = (blk * hi).astype(o.dtype)


def _ep_diff_kvb(acc, aux, outs):
    j = pl.program_id(1)

    @pl.when(j == 0)
    def _():
        outs[0][...] = _seg_rms(acc, aux[0], aux[1], A_DIM)

    @pl.when(j > 0)
    def _():
        outs[0][...] = acc


def _rope_half(b, cos, sin):
    lane = lax.broadcasted_iota(jnp.int32, b.shape, 1)
    half = C_ROPE // 2
    partner = jnp.where(lane < half, pltpu.roll(b, LANES - half, 1), pltpu.roll(b, half, 1))
    return b * cos + partner * sin


def _ep_head_rope(scale, with_kpe):
    def ep(acc, aux, outs):
        if with_kpe:
            g_ref, cos_ref, sin_ref, kpe_ref = aux
            y = jnp.concatenate([acc[:, :LANES], kpe_ref[...]], axis=1)
        else:
            g_ref, cos_ref, sin_ref = aux
            y = acc
        ms = jnp.sum(y * y, axis=-1, keepdims=True) * (1.0 / C_QK)
        yn = y * lax.rsqrt(ms + EPS) * g_ref[...]
        br = _rope_half(yn[:, LANES:], cos_ref[...], sin_ref[...])
        o = jnp.concatenate([yn[:, :LANES], br], axis=1)
        if scale != 1.0:
            o = o * scale
        outs[0][...] = o.astype(outs[0].dtype)
    return ep


def _ep_latent(c_latent):
    def ep(acc, aux, outs):
        c = acc[:, :c_latent]
        ms = jnp.mean(c * c, axis=-1, keepdims=True)
        outs[0][...] = c * lax.rsqrt(ms + EPS) * aux[0][...]
        outs[1][...] = acc[:, c_latent:]
    return ep


def _gelu(x):
    return 0.5 * x * (1.0 + jnp.tanh(math.sqrt(2.0 / math.pi) * (x + 0.044715 * (x * x * x))))


def _ep_gelu(acc, aux, outs):
    outs[0][...] = _gelu(acc)


def _ep_gelu_rms(acc, aux, outs):
    y = _gelu(acc)
    ms = jnp.mean(y * y, axis=-1, keepdims=True)
    outs[0][...] = y * lax.rsqrt(ms + EPS) * aux[0][...]


def _diff_lambda(lam_ref, lam_init):
    lp = lam_ref[...]
    a = jnp.sum(lp[0:1] * lp[1:2], axis=1, keepdims=True)
    b = jnp.sum(lp[2:3] * lp[3:4], axis=1, keepdims=True)
    return jnp.exp(a) - jnp.exp(b) + lam_init


def _diff_combine(o1, o2, lam, gout_ref, lam_init):
    d = o1 - lam * o2
    ms = jnp.mean(d * d, axis=-1, keepdims=True)
    return d * lax.rsqrt(ms + EPS) * gout_ref[...] * (1.0 - lam_init)


def _softmax_probs(s, m_s, l_s, h):
    m_prev = m_s[h]
    m_new = jnp.maximum(m_prev, jnp.max(s, axis=1, keepdims=True))
    alpha = jnp.exp(m_prev - m_new)
    p = jnp.exp(s - _rep_lanes(m_new, s.shape[1]))
    l_s[h] = alpha * l_s[h] + jnp.sum(p, axis=1, keepdims=True)
    m_s[h] = m_new
    return p, alpha


def _softmax_step(s, vt, m_s, l_s, acc_s, h):
    p, alpha = _softmax_probs(s, m_s, l_s, h)
    pv = jnp.dot(p.astype(vt.dtype), vt, preferred_element_type=F32)
    acc_s[h] = _rep_lanes(alpha, pv.shape[1]) * acc_s[h] + pv


def _flash_body(*refs, nhk, g, dq, dv, tq, tk, skv, causal, diff, lam_init):
    if diff:
        q_ref, k_ref, v_ref, lam_ref, gout_ref, o_ref, m_s, l_s, acc_s = refs
    else:
        q_ref, k_ref, v_ref, o_ref, m_s, l_s, acc_s = refs
    qi = pl.program_id(2)
    m_s[...] = jnp.full(m_s.shape, NEG_INF, F32)
    l_s[...] = jnp.zeros(l_s.shape, F32)
    acc_s[...] = jnp.zeros(acc_s.shape, F32)

    def step(ki, masked):
        k0 = pl.multiple_of(ki * tk, tk)
        if masked:
            rows = qi * tq + lax.broadcasted_iota(jnp.int32, (tq, tk), 0)
            cols = k0 + lax.broadcasted_iota(jnp.int32, (tq, tk), 1)
            keep = rows >= cols
        for hk in range(nhk):
            kt = k_ref[pl.ds(k0, tk), hk * dq:(hk + 1) * dq].astype(BF16)
            vt = v_ref[pl.ds(k0, tk), hk * dv:(hk + 1) * dv].astype(BF16)
            for gi in range(g):
                h = hk * g + gi
                s = _nt_dot(q_ref[:, h * dq:(h + 1) * dq].astype(BF16), kt)
                if masked:
                    s = jnp.where(keep, s, NEG_INF)
                _softmax_step(s, vt, m_s, l_s, acc_s, h)

    def loop(lo, hi, masked):
        def body(ki, c):
            step(ki, masked)
            return c
        lax.fori_loop(lo, hi, body, 0)

    if causal:
        n_full = (qi * tq) // tk
        n_all = (qi * tq + tq + tk - 1) // tk
        loop(0, n_full, False)
        loop(n_full, n_all, True)
    elif skv == tk:
        step(0, False)
    else:
        loop(0, skv // tk, False)

    nh = nhk * g
    if diff:
        lam = _diff_lambda(lam_ref, lam_init)
        for j in range(nh // 2):
            o1 = acc_s[2 * j] / l_s[2 * j]
            o2 = acc_s[2 * j + 1] / l_s[2 * j + 1]
            o_ref[:, j * dv:(j + 1) * dv] = _diff_combine(o1, o2, lam, gout_ref, lam_init).astype(o_ref.dtype)
    else:
        for h in range(nh):
            o_ref[:, h * dv:(h + 1) * dv] = (acc_s[h] / l_s[h]).astype(o_ref.dtype)


def _flash(q, k, v, *, name, grid, q_map, k_map, v_map, o_map, nhk, g, dq, dv, tq, tk, skv, causal,
           out_rows, out_cols, out_dtype, diff=None):
    assert dv == LANES
    nh = nhk * g
    n_out_heads = nh // 2 if diff else nh
    in_specs = [pl.BlockSpec((tq, nh * dq), q_map),
                pl.BlockSpec((skv, nhk * dq), k_map),
                pl.BlockSpec((skv, nhk * dv), v_map)]
    args = [q, k, v]
    lam_init = 0.0
    if diff:
        lam_p, gout, lam_init = diff
        in_specs += [pl.BlockSpec(lam_p.shape, lambda b, h, i: (0, 0)),
                     pl.BlockSpec(gout.shape, lambda b, h, i: (0, 0))]
        args += [lam_p, gout]
    body = functools.partial(_flash_body, nhk=nhk, g=g, dq=dq, dv=dv, tq=tq, tk=tk, skv=skv, causal=causal,
                             diff=bool(diff), lam_init=lam_init)
    return pl.pallas_call(
        body, grid=grid, in_specs=in_specs,
        out_specs=pl.BlockSpec((tq, n_out_heads * dv), o_map),
        out_shape=jax.ShapeDtypeStruct((out_rows, out_cols), out_dtype),
        scratch_shapes=[pltpu.VMEM((nh, tq, LANES), F32), pltpu.VMEM((nh, tq, LANES), F32),
                        pltpu.VMEM((nh, tq, dv), F32)],
        compiler_params=_params("parallel", "parallel", "arbitrary"), name=name)(*args)


def _cross_sample_body(q_ref, k_ref, v_ref, o_ref, *, bb, t_new, n_heads, mem_len):
    for i in range(bb):
        for h in range(n_heads):
            r0 = i * mem_len * n_heads + h
            kt = k_ref[pl.ds(r0, mem_len, stride=n_heads), :].astype(BF16)
            vt = v_ref[pl.ds(r0, mem_len, stride=n_heads), :].astype(BF16)
            q = q_ref[pl.ds(i * t_new, t_new), h * X_DIM:(h + 1) * X_DIM].astype(BF16)
            s = _nt_dot(q, kt)
            p = jnp.exp(s - jnp.max(s, axis=1, keepdims=True))
            o = jnp.dot(p.astype(BF16), vt, preferred_element_type=F32) / jnp.sum(p, axis=1, keepdims=True)
            o_ref[pl.ds(i * t_new, t_new), h * X_DIM:(h + 1) * X_DIM] = o


def _cross_sample(qx, mem_k, mem_v, *, layer, mp, db, t_new, n_heads, mem_len, bb=4):
    width = n_heads * X_DIM
    kv_rows = bb * mem_len * n_heads
    r0 = mp // (bb * t_new)
    k0 = layer * (db // bb)
    body = functools.partial(_cross_sample_body, bb=bb, t_new=t_new, n_heads=n_heads, mem_len=mem_len)
    return pl.pallas_call(
        body, grid=(db // bb,),
        in_specs=[pl.BlockSpec((bb * t_new, width), lambda i: (r0 + i, 0)),
                  pl.BlockSpec((kv_rows, X_DIM), lambda i: (k0 + i, 0)),
                  pl.BlockSpec((kv_rows, X_DIM), lambda i: (k0 + i, 0))],
        out_specs=pl.BlockSpec((bb * t_new, width), lambda i: (i, 0)),
        out_shape=jax.ShapeDtypeStruct((db * t_new, width), F32),
        compiler_params=_params("parallel"), name="cross_sample")(qx, mem_k, mem_v)


def _page_copies(caches, dsts, pt_ref, layer, b, c, slot, sems, pp):
    out = []
    for p in range(pp):
        pg = pt_ref[b, c * pp + p]
        for cache, dst in zip(caches, dsts):
            out.append(pltpu.make_async_copy(cache.at[layer, pg], dst(slot, p), sems.at[slot]))
    return out


def _rows_window(buf, n):
    return lambda slot, p: buf.at[slot, pl.ds(p * n, n)]


def _lanes_window(buf, n):
    return lambda slot, p: buf.at[slot, :, pl.ds(p * n, n)]


def _paged_loop(caches, dsts, pt_ref, layer, sems, pp, n_chunks, consume):
    b = pl.program_id(0)
    nb = pl.num_programs(0)

    def start(bb, cc, slot):
        for cp in _page_copies(caches, dsts, pt_ref, layer, bb, cc, slot, sems, pp):
            cp.start()

    @pl.when(b == 0)
    def _():
        start(0, 0, 0)

    def chunk(c, carry):
        slot = (b * n_chunks + c) % 2
        last = c == n_chunks - 1

        @pl.when(jnp.logical_or(jnp.logical_not(last), b + 1 < nb))
        def _():
            start(jnp.where(last, b + 1, b), jnp.where(last, 0, c + 1), 1 - slot)

        for cp in _page_copies(caches, dsts, pt_ref, layer, b, c, slot, sems, pp):
            cp.wait()
        consume(slot, c)
        return carry

    lax.fori_loop(0, n_chunks, chunk, 0)


def _diff_sample_body(pt_ref, q_ref, ck_hbm, cv_hbm, knew_ref, vnew_ref, lam_ref, gout_ref, o_ref,
                      kbuf, vbuf, sems, m_s, l_s, acc_s, *, layer, pp, page, n_chunks, t_new, lam_init):
    m_s[...] = jnp.full(m_s.shape, NEG_INF, F32)
    l_s[...] = jnp.zeros(l_s.shape, F32)
    acc_s[...] = jnp.zeros(acc_s.shape, F32)
    q = q_ref[0]
    rows = q.shape[0]
    half = rows // A_KV_HEADS
    n_pos = pp * page

    def consume(slot, c):
        s = jnp.dot(q, kbuf[slot].astype(BF16), preferred_element_type=F32)
        p, alpha = _softmax_probs(s, m_s, l_s, 0)
        pb = p.astype(BF16)
        pv = [jnp.dot(pb[kh * half:(kh + 1) * half],
                      vbuf[slot, pl.ds(kh, n_pos, stride=A_KV_HEADS), :].astype(BF16),
                      preferred_element_type=F32) for kh in range(A_KV_HEADS)]
        acc_s[0] = alpha * acc_s[0] + jnp.concatenate(pv, axis=0)

    _paged_loop((ck_hbm, cv_hbm), (_lanes_window(kbuf, page), _rows_window(vbuf, A_KV_HEADS * page)),
                pt_ref, layer, sems, pp, n_chunks, consume)

    s = _nt_dot(q.astype(F32), knew_ref[...])
    t_row = lax.broadcasted_iota(jnp.int32, (rows, t_new), 0) % t_new
    l_col = lax.broadcasted_iota(jnp.int32, (rows, t_new), 1)
    s = jnp.where(t_row >= l_col, s, NEG_INF)
    p, alpha = _softmax_probs(s, m_s, l_s, 0)
    vnew = vnew_ref[...]
    pv = [jnp.dot(p[kh * half:(kh + 1) * half], vnew[:, kh * LANES:(kh + 1) * LANES], preferred_element_type=F32)
          for kh in range(A_KV_HEADS)]
    o = (alpha * acc_s[0] + jnp.concatenate(pv, axis=0)) / l_s[0]
    lam = _diff_lambda(lam_ref, lam_init)
    grp = half // 2 // t_new
    for kh in range(A_KV_HEADS):
        ok = o[kh * half:(kh + 1) * half]
        d = _diff_combine(ok[:half // 2], ok[half // 2:], lam, gout_ref, lam_init)
        for gi in range(grp):
            c0 = (kh * grp + gi) * LANES
            o_ref[:, c0:c0 + LANES] = d[gi * t_new:(gi + 1) * t_new].astype(o_ref.dtype)


def _diff_sample(qbd, cache_kt, cache_v, e, page_table, kvb, lam_p, gout, lam_init, *, mp, t_new, pp):
    db, n_pages = page_table.shape
    kw, page = cache_kt.shape[2], cache_kt.shape[3]
    dv = cache_v.shape[3]
    rows = qbd.shape[1]
    assert n_pages % pp == 0 and dv == LANES
    r0 = mp // t_new
    out_w = rows // 2 * LANES // t_new
    in_specs = [pl.BlockSpec((1, rows, kw), lambda b, pt: (b, 0, 0)),
                pl.BlockSpec(memory_space=pl.ANY), pl.BlockSpec(memory_space=pl.ANY),
                pl.BlockSpec((t_new, kw), lambda b, pt: (r0 + b, 0)),
                pl.BlockSpec((t_new, kw), lambda b, pt: (r0 + b, 1)),
                pl.BlockSpec(lam_p.shape, lambda b, pt: (0, 0)),
                pl.BlockSpec(gout.shape, lambda b, pt: (0, 0))]
    gs = pltpu.PrefetchScalarGridSpec(
        num_scalar_prefetch=1, grid=(db,), in_specs=in_specs,
        out_specs=pl.BlockSpec((t_new, out_w), lambda b, pt: (b, 0)),
        scratch_shapes=[pltpu.VMEM((2, kw, pp * page), F32), pltpu.VMEM((2, pp * page * A_KV_HEADS, dv), F32),
                        pltpu.SemaphoreType.DMA((2,)),
                        pltpu.VMEM((1, rows, LANES), F32), pltpu.VMEM((1, rows, LANES), F32),
                        pltpu.VMEM((1, rows, dv), F32)])
    body = functools.partial(_diff_sample_body, layer=e, pp=pp, page=page, n_chunks=n_pages // pp,
                             t_new=t_new, lam_init=lam_init)
    return pl.pallas_call(body, grid_spec=gs, out_shape=jax.ShapeDtypeStruct((db * t_new, out_w), F32),
                          compiler_params=_params("arbitrary"), name="diff_sample")(
                              page_table, qbd, cache_kt, cache_v, kvb, kvb, lam_p, gout)


def _pool_finish(d, g, w_ref, scale_ref, gd):
    y = jnp.dot(d.astype(BF16), w_ref[g], preferred_element_type=F32)
    return y * scale_ref[:, g * gd:(g + 1) * gd]


def _pool_prompt_body(*refs, tm, gd, n_grp):
    cur = refs[:n_grp]
    prev = refs[n_grp:2 * n_grp]
    w_ref, scale_ref, o_ref, ext = refs[2 * n_grp:]
    i = pl.program_id(1)
    hist = 16
    pos = i * tm + lax.broadcasted_iota(jnp.int32, (tm, 1), 0)
    for g, w in enumerate(B_WINDOWS):
        x = cur[g][...]
        ext[pl.ds(hist, tm), :] = x
        ext[pl.ds(0, hist), :] = jnp.where(i > 0, prev[g][...], 0.0)
        s = x
        for k in range(1, w):
            s = s + ext[pl.ds(hist - k, tm), :]
        cnt = jnp.minimum(pos + 1, w).astype(F32)
        d = s / cnt - x
        o_ref[:, g * gd:(g + 1) * gd] = _pool_finish(d, g, w_ref, scale_ref, gd).astype(o_ref.dtype)


def _pool_prompt(kvb, col0, w_lin, scale, *, batch, seq, rows_out, tm=512):
    n_grp, gd, _ = w_lin.shape
    nt = seq // tm
    c0 = col0 // gd
    hist = 16
    in_specs = [pl.BlockSpec((tm, gd), functools.partial(lambda b, i, g: (b * nt + i, c0 + g), g=g))
                for g in range(n_grp)]
    in_specs += [pl.BlockSpec((hist, gd), functools.partial(
        lambda b, i, g: (jnp.maximum((b * nt + i) * (tm // hist) - 1, 0), c0 + g), g=g)) for g in range(n_grp)]
    in_specs += [pl.BlockSpec(w_lin.shape, lambda b, i: (0, 0, 0)),
                 pl.BlockSpec(scale.shape, lambda b, i: (0, 0))]
    body = functools.partial(_pool_prompt_body, tm=tm, gd=gd, n_grp=n_grp)
    return pl.pallas_call(
        body, grid=(batch, nt), in_specs=in_specs,
        out_specs=pl.BlockSpec((tm, n_grp * gd), lambda b, i: (b * nt + i, 0)),
        out_shape=jax.ShapeDtypeStruct((rows_out, n_grp * gd), BF16),
        scratch_shapes=[pltpu.VMEM((tm + hist, gd), F32)],
        compiler_params=_params("parallel", "arbitrary"), name="pool_prompt")(
            *([kvb] * (2 * n_grp)), w_lin, scale)


def _pool_sample_body(ext_ref, w_ref, scale_ref, o_ref, *, bb, t_new, gd, past):
    pos = past + lax.broadcasted_iota(jnp.int32, (1, t_new, 1), 1)
    for g, w in enumerate(B_WINDOWS):
        x = ext_ref[:, pl.ds(B_HIST, t_new), pl.ds(g * gd, gd)]
        s = x
        for k in range(1, w):
            s = s + ext_ref[:, pl.ds(B_HIST - k, t_new), pl.ds(g * gd, gd)]
        cnt = jnp.minimum(pos + 1, w).astype(F32)
        d = (s / cnt - x).reshape(bb * t_new, gd)
        o_ref[:, g * gd:(g + 1) * gd] = _pool_finish(d, g, w_ref, scale_ref, gd).astype(o_ref.dtype)


def _pool_sample(ext, w_lin, scale, *, past, bb=16):
    db, n_ext, width = ext.shape
    t_new = n_ext - B_HIST
    n_grp, gd, _ = w_lin.shape
    body = functools.partial(_pool_sample_body, bb=bb, t_new=t_new, gd=gd, past=past)
    return pl.pallas_call(
        body, grid=(db // bb,),
        in_specs=[pl.BlockSpec((bb, n_ext, width), lambda i: (i, 0, 0)),
                  pl.BlockSpec(w_lin.shape, lambda i: (0, 0, 0)),
                  pl.BlockSpec(scale.shape, lambda i: (0, 0))],
        out_specs=pl.BlockSpec((bb * t_new, width), lambda i: (i, 0)),
        out_shape=jax.ShapeDtypeStruct((db * t_new, width), BF16),
        compiler_params=_params("parallel"), name="pool_sample")(ext, w_lin, scale)


def _chunk_gate_body(u_ref, v_ref, w_ref, b_ref, o_ref, *, gd, n_grp):
    for g in range(n_grp):
        cs = slice(g * gd, (g + 1) * gd)
        mix = jnp.dot(w_ref[g], v_ref[:, cs].astype(BF16), preferred_element_type=F32) + b_ref[:, cs]
        o_ref[:, cs] = (u_ref[:, cs] * mix).astype(o_ref.dtype)


def _chunk_gate(u, v, wmix, bias, *, row0, rows, name):
    width = u.shape[1]
    n_grp, r, _ = wmix.shape
    gd = width // n_grp
    r0 = row0 // r
    body = functools.partial(_chunk_gate_body, gd=gd, n_grp=n_grp)
    return pl.pallas_call(
        body, grid=(rows // r,),
        in_specs=[pl.BlockSpec((r, width), lambda i: (r0 + i, 0)),
                  pl.BlockSpec((r, width), lambda i: (r0 + i, 0)),
                  pl.BlockSpec(wmix.shape, lambda i: (0, 0, 0)),
                  pl.BlockSpec(bias.shape, lambda i: (0, 0))],
        out_specs=pl.BlockSpec((r, width), lambda i: (i, 0)),
        out_shape=jax.ShapeDtypeStruct((rows, width), BF16),
        compiler_params=_params("parallel"), name=name)(u, v, wmix, bias)


def _absorb_body(q_ref, w_ref, gk_ref, gk2_ref, qabs_ref, qrot_ref, *, tb, t_new):
    q = q_ref[...].astype(F32)
    qn = (q[:, :C_NOPE] * gk_ref[...]).astype(BF16)
    y = jnp.dot(qn, w_ref[0], preferred_element_type=F32)
    qabs_ref[...] = y.reshape(tb, 1, t_new, y.shape[1])
    pe = q[:, C_NOPE:]
    lane = lax.broadcasted_iota(jnp.int32, pe.shape, 1)
    half = C_ROPE // 2
    swapped = jnp.where(lane < half, pltpu.roll(pe, LANES - half, 1), -pltpu.roll(pe, half, 1))
    swapped = jnp.where(lane < C_ROPE, swapped, 0.0)
    qrot = (pe + pltpu.roll(swapped, C_ROPE, 1)) * gk2_ref[...]
    qrot_ref[...] = qrot.reshape(tb, 1, t_new, LANES)


def _absorb(qh, wuk_t, gk_nope, gk_rope2, *, mp, db, t_new, tb=16):
    n_heads, _, c_latent = wuk_t.shape
    tm = tb * t_new
    r0 = mp // tm
    body = functools.partial(_absorb_body, tb=tb, t_new=t_new)
    shp = lambda w: jax.ShapeDtypeStruct((db, n_heads, t_new, w), F32)
    spec = lambda w: pl.BlockSpec((tb, 1, t_new, w), lambda i, h: (i, h, 0, 0))
    return pl.pallas_call(
        body, grid=(db // tb, n_heads),
        in_specs=[pl.BlockSpec((tm, C_HEAD_PAD), lambda i, h: (r0 + i, h)),
                  pl.BlockSpec((1, C_NOPE, c_latent), lambda i, h: (h, 0, 0)),
                  pl.BlockSpec((1, C_NOPE), lambda i, h: (0, 0)),
                  pl.BlockSpec((1, LANES), lambda i, h: (0, 0))],
        out_specs=[spec(c_latent), spec(LANES)],
        out_shape=[shp(c_latent), shp(LANES)],
        compiler_params=_params("parallel", "arbitrary"), name="mla_absorb")(qh, wuk_t, gk_nope, gk_rope2)


def _mla_sample_body(pt_ref, qabs_ref, qrot_ref, wuk_ref, tabt_ref, cc_hbm, ckt_hbm, cnew_ref, knew_ref,
                     tabn_ref, o_ref, cbuf, kbuf, sems, lhs_s, cb_s, s_s, m_s, l_s, acc_s, *,
                     layer, pp, page, n_chunks, tl, t_new, n_heads):
    b = pl.program_id(0)
    n_up = wuk_ref.shape[0]
    hd = n_up // n_heads
    n_pos = pp * page
    c_latent = cbuf.shape[2]

    @pl.when(b == 0)
    def _():
        lhs_s[pl.ds(0, n_up), :] = wuk_ref[...]

    lhs_s[pl.ds(n_up, n_heads * t_new), :] = qabs_ref[0].astype(BF16)
    m_s[...] = jnp.full(m_s.shape, NEG_INF, F32)
    l_s[...] = jnp.zeros(l_s.shape, F32)
    acc_s[...] = jnp.zeros(acc_s.shape, F32)
    qrot = qrot_ref[0]
    qrot_b = qrot.astype(BF16)

    def scores(big, ksq, s_pe):
        lk = big.shape[1]
        kn = big[:n_up]
        nsq = jnp.sum((kn * kn).reshape(n_heads, hd, lk), axis=1)
        r = lax.rsqrt((nsq + ksq) * (1.0 / C_QK) + EPS)
        r_rows = jnp.concatenate([jnp.broadcast_to(r[h:h + 1], (t_new, lk)) for h in range(n_heads)], axis=0)
        return (big[n_up:] + s_pe) * r_rows

    def consume(slot, c):
        lhs = lhs_s[...]
        for t in range(n_pos // tl):
            cb = cbuf[slot, pl.ds(t * tl, tl), :].astype(BF16)
            cb_s[pl.ds(t * tl, tl), :] = cb
            kt = kbuf[slot, :, pl.ds(t * tl, tl)]
            l0 = pl.multiple_of(c * n_pos + t * tl, tl)
            krot = jnp.concatenate([kt, kt], axis=0) * tabt_ref[:, pl.ds(l0, tl)]
            s_pe = jnp.dot(qrot_b, krot.astype(BF16), preferred_element_type=F32)
            ksq = jnp.sum(kt * kt, axis=0, keepdims=True)
            s_s[:, pl.ds(t * tl, tl)] = scores(_nt_dot(lhs, cb), ksq, s_pe)
        p, alpha = _softmax_probs(s_s[...], m_s, l_s, 0)
        pv = jnp.dot(p.astype(BF16), cb_s[...], preferred_element_type=F32)
        acc_s[0] = _rep_lanes(alpha, c_latent) * acc_s[0] + pv

    _paged_loop((cc_hbm, ckt_hbm), (_rows_window(cbuf, page), _lanes_window(kbuf, page)),
                pt_ref, layer, sems, pp, n_chunks, consume)

    rows = n_heads * t_new
    cn = cnew_ref[...]
    kpe = knew_ref[:, :C_ROPE]
    krot = jnp.concatenate([kpe, kpe], axis=1) * tabn_ref[...]
    ksq = _nt_dot(jnp.ones((8, C_ROPE), F32), kpe * kpe)[:1]
    s = scores(_nt_dot(lhs_s[...].astype(F32), cn), ksq, _nt_dot(qrot, krot))
    t_row = lax.broadcasted_iota(jnp.int32, (rows, t_new), 0) % t_new
    l_col = lax.broadcasted_iota(jnp.int32, (rows, t_new), 1)
    s = jnp.where(t_row >= l_col, s, NEG_INF)
    p, alpha = _softmax_probs(s, m_s, l_s, 0)
    pv = jnp.dot(p, cn, preferred_element_type=F32)
    o_ref[0] = (_rep_lanes(alpha, c_latent) * acc_s[0] + pv) / _rep_lanes(l_s[0], c_latent)


def _mla_sample(qabs, qrot, wuk2, tab_t, cache_c, cache_kpe_t, o, page_table, c_all, kpe_all, tab_new, *,
                mp, t_new, pp, tl=256):
    db, n_pages = page_table.shape
    page, c_latent = cache_c.shape[2], cache_c.shape[3]
    rows = qabs.shape[1]
    n_heads = rows // t_new
    n_up = wuk2.shape[0]
    n_pos = pp * page
    assert n_pages % pp == 0 and n_pos % tl == 0
    r0 = mp // t_new
    in_specs = [pl.BlockSpec((1, rows, c_latent), lambda b, pt: (b, 0, 0)),
                pl.BlockSpec((1, rows, LANES), lambda b, pt: (b, 0, 0)),
                pl.BlockSpec(wuk2.shape, lambda b, pt: (0, 0)),
                pl.BlockSpec(tab_t.shape, lambda b, pt: (0, 0)),
                pl.BlockSpec(memory_space=pl.ANY), pl.BlockSpec(memory_space=pl.ANY),
                pl.BlockSpec((t_new, c_latent), lambda b, pt: (r0 + b, 0)),
                pl.BlockSpec((t_new, LANES), lambda b, pt: (r0 + b, 0)),
                pl.BlockSpec(tab_new.shape, lambda b, pt: (0, 0))]
    gs = pltpu.PrefetchScalarGridSpec(
        num_scalar_prefetch=1, grid=(db,), in_specs=in_specs,
        out_specs=pl.BlockSpec((1, rows, c_latent), lambda b, pt: (b, 0, 0)),
        scratch_shapes=[pltpu.VMEM((2, n_pos, c_latent), F32), pltpu.VMEM((2, C_ROPE, n_pos), F32),
                        pltpu.SemaphoreType.DMA((2,)),
                        pltpu.VMEM((n_up + rows, c_latent), BF16),
                        pltpu.VMEM((n_pos, c_latent), BF16), pltpu.VMEM((rows, n_pos), F32),
                        pltpu.VMEM((1, rows, LANES), F32), pltpu.VMEM((1, rows, LANES), F32),
                        pltpu.VMEM((1, rows, c_latent), F32)])
    body = functools.partial(_mla_sample_body, layer=o, pp=pp, page=page, n_chunks=n_pages // pp, tl=tl,
                             t_new=t_new, n_heads=n_heads)
    return pl.pallas_call(body, grid_spec=gs, out_shape=jax.ShapeDtypeStruct((db, rows, c_latent), F32),
                          compiler_params=_params("arbitrary"), name="mla_sample")(
                              page_table, qabs, qrot, wuk2, tab_t, cache_c, cache_kpe_t, c_all, kpe_all, tab_new)


def _mla_up_body(ctx_ref, w_ref, o_ref, *, tb, t_new):
    c = ctx_ref[...].reshape(tb * t_new, ctx_ref.shape[3]).astype(BF16)
    o_ref[...] = jnp.dot(c, w_ref[0], preferred_element_type=F32).astype(o_ref.dtype)


def _mla_up(ctx4, wuv_h, *, tb=16):
    db, n_heads, t_new, c_latent = ctx4.shape
    dv = wuv_h.shape[2]
    body = functools.partial(_mla_up_body, tb=tb, t_new=t_new)
    return pl.pallas_call(
        body, grid=(db // tb, n_heads),
        in_specs=[pl.BlockSpec((tb, 1, t_new, c_latent), lambda i, h: (i, h, 0, 0)),
                  pl.BlockSpec((1, c_latent, dv), lambda i, h: (h, 0, 0))],
        out_specs=pl.BlockSpec((tb * t_new, dv), lambda i, h: (i, h)),
        out_shape=jax.ShapeDtypeStruct((db * t_new, n_heads * dv), BF16),
        compiler_params=_params("parallel", "arbitrary"), name="mla_up")(ctx4, wuv_h)


def _router_body(x_ref, g_ref, w_ref, b_ref, h_ref, r_ref, *, n_groups, per_group):
    xf = x_ref[...]
    h = xf * lax.rsqrt(jnp.mean(xf * xf, axis=-1, keepdims=True) + EPS) * g_ref[...]
    h_ref[...] = h
    hh = h.astype(BF16)
    hl = (h - hh.astype(F32)).astype(BF16)
    w = w_ref[...]
    y1 = jnp.dot(hh, w, preferred_element_type=F32)
    y2 = jnp.dot(hl, w[:, :LANES], preferred_element_type=F32)
    logits = y1[:, :LANES] + y1[:, LANES:] + y2 + b_ref[...]
    lane = lax.broadcasted_iota(jnp.int32, logits.shape, 1)
    big = jnp.int32(1 << 20)
    low = -3.0e38
    n_e = n_groups * per_group
    is_g = lane < n_groups
    gl = jnp.where(is_g, logits, low)
    gmax = jnp.max(gl, axis=1, keepdims=True)
    gsel = jnp.min(jnp.where(gl == gmax, lane, big), axis=1, keepdims=True)
    gsum = jnp.sum(jnp.where(is_g, jnp.exp(gl - gmax), 0.0), axis=1, keepdims=True)
    gprob = 1.0 / gsum
    e_lane = lane - n_groups
    e_grp = jnp.right_shift(e_lane, per_group.bit_length() - 1)
    in_grp = jnp.logical_and(jnp.logical_and(e_lane >= 0, e_lane < n_e), e_grp == gsel)
    el = jnp.where(in_grp, logits, low)
    t1 = jnp.max(el, axis=1, keepdims=True)
    i1 = jnp.min(jnp.where(el == t1, lane, big), axis=1, keepdims=True)
    el2 = jnp.where(lane == i1, low, el)
    t2 = jnp.max(el2, axis=1, keepdims=True)
    i2 = jnp.min(jnp.where(el2 == t2, lane, big), axis=1, keepdims=True)
    ex = jnp.exp(t2 - t1)
    w1 = gprob / (1.0 + ex)
    w2 = gprob * ex / (1.0 + ex)
    out = jnp.where(lane == 0, (i1 - n_groups).astype(F32), 0.0)
    out = jnp.where(lane == 1, (i2 - n_groups).astype(F32), out)
    out = jnp.where(lane == 2, w1, out)
    out = jnp.where(lane == 3, w2, out)
    r_ref[...] = out


def _router(x, gain, wcat, bias, *, n_groups, per_group, tm=512):
    rows, d = x.shape
    body = functools.partial(_router_body, n_groups=n_groups, per_group=per_group)
    return pl.pallas_call(
        body, grid=(rows // tm,),
        in_specs=[pl.BlockSpec((tm, d), lambda i: (i, 0)), pl.BlockSpec((1, d), lambda i: (0, 0)),
                  pl.BlockSpec(wcat.shape, lambda i: (0, 0)), pl.BlockSpec(bias.shape, lambda i: (0, 0))],
        out_specs=[pl.BlockSpec((tm, d), lambda i: (i, 0)), pl.BlockSpec((tm, LANES), lambda i: (i, 0))],
        out_shape=[jax.ShapeDtypeStruct((rows, d), F32), jax.ShapeDtypeStruct((rows, LANES), F32)],
        compiler_params=_params("parallel"), name="moe_router")(x, gain, wcat, bias)


def _gather_copy(src_hbm, idx_ref, base, r, dst, sem):
    return pltpu.make_async_copy(src_hbm.at[pl.ds(idx_ref[base + r], 1)], dst.at[pl.ds(r, 1)], sem)


def _start_gather(src_hbm, idx_ref, base, n, dst, sem):
    def body(r, c):
        _gather_copy(src_hbm, idx_ref, base, r, dst, sem).start()
        return c
    lax.fori_loop(0, n, body, 0, unroll=8)


def _wait_gather(src_hbm, idx_ref, base, n, dst, sem):
    def body(r, c):
        _gather_copy(src_hbm, idx_ref, base, r, dst, sem).wait()
        return c
    lax.fori_loop(0, n, body, 0, unroll=8)


def _experts_body(te_ref, nt_ref, tok_ref, h_hbm, w1_ref, w3_ref, w2_ref, o_ref, xbuf, sems, w1b, w3b, w2b, *, tm):
    t = pl.program_id(0)
    nt = nt_ref[0]
    slot = t % 2

    @pl.when(jnp.logical_and(t == 0, nt > 0))
    def _():
        _start_gather(h_hbm, tok_ref, 0, tm, xbuf.at[0], sems.at[0])

    @pl.when(t + 1 < nt)
    def _():
        _start_gather(h_hbm, tok_ref, (t + 1) * tm, tm, xbuf.at[1 - slot], sems.at[1 - slot])

    @pl.when(t < nt)
    def _():
        changed = jnp.logical_or(t == 0, te_ref[t] != te_ref[jnp.maximum(t - 1, 0)])

        @pl.when(changed)
        def _():
            w1b[...] = w1_ref[0, 0].astype(BF16)
            w3b[...] = w3_ref[0, 0].astype(BF16)
            w2b[...] = w2_ref[0, 0].astype(BF16)

        _wait_gather(h_hbm, tok_ref, t * tm, tm, xbuf.at[slot], sems.at[slot])
        x = xbuf[slot].astype(BF16)
        a = jnp.dot(x, w1b[...], preferred_element_type=F32)
        b = jnp.dot(x, w3b[...], preferred_element_type=F32)
        hid = (a * jax.nn.sigmoid(a)) * b
        o_ref[...] = jnp.dot(hid.astype(BF16), w2b[...], preferred_element_type=F32)

    @pl.when(t >= nt)
    def _():
        o_ref[...] = jnp.zeros(o_ref.shape, o_ref.dtype)


def _experts(h, tile_e, n_tiles, slot_tok, w1, w3, w2, *, layer, tm):
    n_slots = slot_tok.shape[0]
    t_max = n_slots // tm
    d = h.shape[1]
    f = w1.shape[3]
    gs = pltpu.PrefetchScalarGridSpec(
        num_scalar_prefetch=3, grid=(t_max,),
        in_specs=[pl.BlockSpec(memory_space=pl.ANY),
                  pl.BlockSpec((1, 1, d, f), lambda t, te, nt, tok: (layer, te[t], 0, 0)),
                  pl.BlockSpec((1, 1, d, f), lambda t, te, nt, tok: (layer, te[t], 0, 0)),
                  pl.BlockSpec((1, 1, f, d), lambda t, te, nt, tok: (layer, te[t], 0, 0))],
        out_specs=pl.BlockSpec((tm, d), lambda t, te, nt, tok: (t, 0)),
        scratch_shapes=[pltpu.VMEM((2, tm, d), F32), pltpu.SemaphoreType.DMA((2,)),
                        pltpu.VMEM((d, f), BF16), pltpu.VMEM((d, f), BF16), pltpu.VMEM((f, d), BF16)])
    body = functools.partial(_experts_body, tm=tm)
    return pl.pallas_call(body, grid_spec=gs, out_shape=jax.ShapeDtypeStruct((n_slots, d), F32),
                          compiler_params=_params("arbitrary"), name="moe_experts")(
                              tile_e, n_tiles, slot_tok, h, w1, w3, w2)


def _combine_body(slot_ref, x_ref, r_ref, y_hbm, o_ref, buf, sems, *, tm, top_k):
    i = pl.program_id(0)
    n = pl.num_programs(0)
    slot = i % 2
    nrow = tm * top_k

    @pl.when(i == 0)
    def _():
        _start_gather(y_hbm, slot_ref, 0, nrow, buf.at[0], sems.at[0])

    @pl.when(i + 1 < n)
    def _():
        _start_gather(y_hbm, slot_ref, (i + 1) * nrow, nrow, buf.at[1 - slot], sems.at[1 - slot])

    _wait_gather(y_hbm, slot_ref, i * nrow, nrow, buf.at[slot], sems.at[slot])
    acc = x_ref[...]
    r = r_ref[...]
    for k in range(top_k):
        acc = acc + r[:, top_k + k:top_k + k + 1] * buf[slot, pl.ds(k * tm, tm), :]
    o_ref[...] = acc


def _combine(x, route, ys, slots, *, tm=256):
    rows, d = x.shape
    gs = pltpu.PrefetchScalarGridSpec(
        num_scalar_prefetch=1, grid=(rows // tm,),
        in_specs=[pl.BlockSpec((tm, d), lambda i, s: (i, 0)), pl.BlockSpec((tm, LANES), lambda i, s: (i, 0)),
                  pl.BlockSpec(memory_space=pl.ANY)],
        out_specs=pl.BlockSpec((tm, d), lambda i, s: (i, 0)),
        scratch_shapes=[pltpu.VMEM((2, tm * TOP_K, d), F32), pltpu.SemaphoreType.DMA((2,))])
    body = functools.partial(_combine_body, tm=tm, top_k=TOP_K)
    return pl.pallas_call(body, grid_spec=gs, out_shape=jax.ShapeDtypeStruct((rows, d), F32),
                          compiler_params=_params("arbitrary"), name="moe_combine")(slots, x, route, ys)


def _moe_plan(route, n_experts, tm, comb_tm):
    rows = route.shape[0]
    ids = route[:, :TOP_K].astype(jnp.int32)
    flat = ids.reshape(-1)
    n_assign = flat.shape[0]
    order = jnp.argsort(flat, stable=True).astype(jnp.int32)
    counts = jnp.zeros((n_experts,), jnp.int32).at[flat].add(1)
    tiles = (counts + tm - 1) // tm
    tile_end = jnp.cumsum(tiles)
    pad_off = (tile_end - tiles) * tm
    sort_off = jnp.cumsum(counts) - counts
    e_sorted = flat[order]
    slot_sorted = pad_off[e_sorted] + jnp.arange(n_assign, dtype=jnp.int32) - sort_off[e_sorted]
    t_max = n_assign // tm + n_experts
    slot_tok = jnp.zeros((t_max * tm,), jnp.int32).at[slot_sorted].set(order // TOP_K)
    slot_of = jnp.zeros((n_assign,), jnp.int32).at[order].set(slot_sorted).reshape(rows, TOP_K)
    n_tiles = tile_end[-1]
    t_idx = jnp.arange(t_max, dtype=jnp.int32)
    tile_e = jnp.searchsorted(tile_end, jnp.minimum(t_idx, n_tiles - 1), side="right").astype(jnp.int32)
    tile_e = jnp.minimum(tile_e, n_experts - 1)
    slots = slot_of.reshape(rows // comb_tm, comb_tm, TOP_K).transpose(0, 2, 1).reshape(-1)
    return tile_e, n_tiles.reshape(1).astype(jnp.int32), slot_tok, slots


def _hmoe(x, gain, wg, bg, we, be, w1, w3, w2, layer):
    d = x.shape[1]
    n_groups, per_group = we.shape[1], we.shape[2]
    n_experts = n_groups * per_group
    wr = jnp.concatenate([wg, we.reshape(d, n_experts)], axis=1)
    wr = jnp.pad(wr, ((0, 0), (0, LANES - wr.shape[1])))
    wr_hi = wr.astype(BF16)
    wr_lo = (wr - wr_hi.astype(F32)).astype(BF16)
    wcat = jnp.concatenate([wr_hi, wr_lo], axis=1)
    bias = jnp.pad(jnp.concatenate([bg, be.reshape(-1)]), (0, LANES - n_groups - n_experts)).reshape(1, LANES)
    h, route = _router(x, gain, wcat, bias.astype(F32), n_groups=n_groups, per_group=per_group)
    comb_tm = 256
    tile_e, n_tiles, slot_tok, slots = _moe_plan(route, n_experts, MOE_TILE, comb_tm)
    ys = _experts(h, tile_e, n_tiles, slot_tok, w1, w3, w2, layer=layer, tm=MOE_TILE)
    return _combine(x, route, ys, slots, tm=comb_tm)


def _rope_tables(pos, width, first_half_sign):
    half = C_ROPE // 2
    inv = ROPE_THETA ** (-np.arange(half, dtype=np.float64) * 2.0 / C_ROPE)
    ang = np.asarray(pos, np.float64)[:, None] * inv[None, :]
    cos = np.zeros((len(pos), width), np.float32)
    sin = np.zeros((len(pos), width), np.float32)
    cos[:, :half] = np.cos(ang)
    cos[:, half:C_ROPE] = np.cos(ang)
    sin[:, :half] = first_half_sign * np.sin(ang)
    sin[:, half:C_ROPE] = np.sin(ang)
    return cos, sin


def kernel(x_prompt, x_sample, cache_a_k, cache_a_v, state_pool_hist, cache_c_latent, cache_c_kpe, cache_mem_k, cache_mem_v, page_table, mem_prompt, norm_mix, norm_mem, norm_ffn, l0_w_in, l0_w_out, a_gq, a_gk, a_lam, a_g_out, b_w, b_scale, l1_w_in, l1_w_out, c_g_latent, c_w_uk, c_w_uv, c_gq, c_gk, d_g_v, d_w_s, d_b_s, x_wq, x_wk, x_wv, x_gq, x_gk, x_wo, moe_wg, moe_bg, moe_we, moe_be, moe_w1, moe_w3, moe_w2):
    batch, seq, d = x_prompt.shape
    db, t_new, _ = x_sample.shape
    depth = norm_mix.shape[0]
    n_pool, page = cache_a_k.shape[1], cache_a_k.shape[2]
    n_pages = page_table.shape[1]
    past = n_pages * page
    mem_len = mem_prompt.shape[1]
    mp, ms = batch * seq, db * t_new
    m = mp + ms
    a_heads = l0_w_out.shape[1] // 2 // (2 * A_DIM)
    a_group = a_heads // A_KV_HEADS
    a_q = a_heads * 2 * A_DIM
    a_k = A_KV_HEADS * 2 * A_DIM
    b_width = b_scale.shape[1]
    c_latent = c_w_uk.shape[1]
    c_heads = c_w_uk.shape[2]
    c_v = c_w_uv.shape[3]
    d_width = d_g_v.shape[1] * d_g_v.shape[2]
    x_heads = x_wq.shape[2] // X_DIM
    x_width = x_heads * X_DIM
    tq = 256
    nq = seq // tq
    pp = 16

    x = jnp.concatenate([x_prompt.reshape(mp, d), x_sample.reshape(ms, d)], axis=0)
    pos_all = np.concatenate([np.tile(np.arange(seq), batch), np.tile(past + np.arange(t_new), db)])
    cos_all, sin_all = _rope_tables(pos_all, LANES, -1.0)
    key_tab_t = jnp.asarray(np.concatenate(_rope_tables(np.arange(past), C_ROPE, 1.0), axis=1).T.copy())
    new_tab = jnp.asarray(np.concatenate(_rope_tables(past + np.arange(t_new), C_ROPE, 1.0), axis=1))

    ak_l, av_l, pl_l, cl_l, ck_l, mk_l, mv_l, dv_l = [], [], [], [], [], [], [], []

    for layer in range(depth):
        gmix = norm_mix[layer].reshape(1, d)
        if layer % 2 == 0:
            e = layer // 2
            lam_init = 0.8 - 0.6 * math.exp(-0.3 * layer)
            w_in = l0_w_in[e]
            ones64 = _block_ones(256, A_DIM)
            gq_t = jnp.tile(a_gq[e].reshape(1, 2 * A_DIM), (1, 2))
            gk_t = jnp.tile(a_gk[e].reshape(1, 2 * A_DIM), (1, 2))
            (q_pad,) = _matmul([x], [w_in[:, :a_q].astype(BF16)], _ep_diff_q, [(2 * a_q, BF16, 512)],
                               name="l0_q", gain=gmix, aux=[(ones64, "const"), (gq_t, "const")])
            (kvb,) = _matmul([x], [w_in[:, a_q:].astype(BF16)], _ep_diff_kvb, [(2 * a_k + b_width, F32, 256)],
                             name="l0_kvb", gain=gmix, aux=[(ones64, "const"), (gk_t, "const")])
            lam_p = a_lam[e]
            gout = a_g_out[e].reshape(1, 2 * A_DIM)
            n_maps = a_group * 2
            oa = _flash(q_pad, kvb, kvb, name="diff_prompt", grid=(batch, A_KV_HEADS, nq),
                        q_map=lambda b, h, i: (b * nq + i, h), k_map=lambda b, h, i: (b, h),
                        v_map=lambda b, h, i: (b, A_KV_HEADS + h), o_map=lambda b, h, i: (b * nq + i, h),
                        nhk=1, g=n_maps, dq=LANES, dv=LANES, tq=tq, tk=512, skv=seq, causal=True,
                        out_rows=mp, out_cols=a_q, out_dtype=F32, diff=(lam_p, gout, lam_init))
            qs = q_pad[mp:].reshape(db, t_new, A_KV_HEADS, a_group, 2, LANES).transpose(0, 2, 4, 3, 1, 5)
            eye = jnp.eye(A_KV_HEADS, dtype=BF16)[None, :, None, None, None, :, None]
            qbd = (qs[:, :, :, :, :, None, :] * eye).reshape(db, A_KV_HEADS * n_maps * t_new, A_KV_HEADS * LANES)
            ck_t = jnp.transpose(cache_a_k, (0, 1, 3, 4, 5, 2)).reshape(cache_a_k.shape[0], n_pool, a_k, page)
            cv = cache_a_v.reshape(cache_a_v.shape[0], n_pool, page * A_KV_HEADS, 2 * A_DIM)
            oa_s = _diff_sample(qbd, ck_t, cv, e, page_table, kvb, lam_p, gout, lam_init, mp=mp, t_new=t_new, pp=pp)
            b_wb = b_w[e].astype(BF16)
            b_sc = b_scale[e].reshape(1, b_width)
            ob = _pool_prompt(kvb, 2 * a_k, b_wb, b_sc, batch=batch, seq=seq, rows_out=mp)
            hb_s = kvb[mp:, 2 * a_k:].reshape(db, t_new, b_width)
            ext_s = jnp.concatenate([state_pool_hist[e], hb_s], axis=1)
            ob_s = _pool_sample(ext_s, b_wb, b_sc, past=past)
            w_out = l0_w_out[e].astype(BF16)
            (x,) = _matmul([(oa, oa_s), (ob, ob_s)], [w_out[:a_q], w_out[a_q:]], _ep_residual, [(d, F32, 512)],
                           name="l0_out",
                           aux=[(x, "tile")], tn=512)
            kvb_p = kvb[:mp].reshape(batch, seq, -1)
            kvb_s = kvb[mp:].reshape(db, t_new, -1)
            ak_l.append((kvb_p[..., :a_k].reshape(batch, seq, A_KV_HEADS, 2, A_DIM),
                         kvb_s[..., :a_k].reshape(db, t_new, A_KV_HEADS, 2, A_DIM)))
            av_l.append((kvb_p[..., a_k:2 * a_k].reshape(batch, seq, A_KV_HEADS, 2 * A_DIM),
                         kvb_s[..., a_k:2 * a_k].reshape(db, t_new, A_KV_HEADS, 2 * A_DIM)))
            pl_l.append((kvb_p[:, seq - B_HIST:, 2 * a_k:], ext_s[:, t_new:]))
        else:
            o = layer // 2
            w_in = l1_w_in[o]
            o1 = c_heads * C_QK
            o2 = o1 + c_latent
            o3 = o2 + C_ROPE
            o4 = o3 + d_width
            pad_h = C_HEAD_PAD - C_QK
            wq = jnp.pad(w_in[:, :o1].reshape(d, c_heads, C_QK), ((0, 0), (0, 0), (0, pad_h)))
            wq = wq.reshape(d, c_heads * C_HEAD_PAD).astype(BF16)
            gq_pad = jnp.pad(c_gq[o], (0, pad_h)).reshape(1, C_HEAD_PAD)
            gk_pad = jnp.pad(c_gk[o], (0, pad_h)).reshape(1, C_HEAD_PAD)
            rope_aux = lambda g: [(g, "const"), (jnp.asarray(cos_all), "row"), (jnp.asarray(sin_all), "row")]
            (qh,) = _matmul([x], [wq], _ep_head_rope(C_QK ** -0.5, False), [(c_heads * C_HEAD_PAD, BF16, 256)],
                            name="l1_q", gain=gmix, aux=rope_aux(gq_pad))
            wck = jnp.pad(w_in[:, o1:o3], ((0, 0), (0, LANES - C_ROPE))).astype(BF16)
            c_all, kpe_all = _matmul([x], [wck], _ep_latent(c_latent),
                                     [(c_latent, F32, c_latent), (LANES, F32, LANES)], name="l1_ckpe", gain=gmix,
                                     aux=[(c_g_latent[o].reshape(1, c_latent), "const")],
                                     tn=c_latent + LANES)
            (u,) = _matmul([x], [w_in[:, o3:o4].astype(BF16)], _ep_gelu, [(d_width, F32, 256)], name="l1_u",
                           gain=gmix)
            gvw = d_g_v.shape[2]
            (gv,) = _matmul([x], [w_in[:, o4:].astype(BF16)], _ep_gelu_rms, [(d_width, F32, gvw)], name="l1_v",
                            gain=gmix, tn=gvw,
                            aux=[(d_g_v[o].reshape(1, d_width), "col")])
            wuk_pad = jnp.pad(c_w_uk[o], ((0, 0), (0, 0), (0, C_HEAD_PAD - C_NOPE)))
            wuk_pad = wuk_pad.reshape(c_latent, c_heads * C_HEAD_PAD).astype(BF16)
            (kh,) = _matmul([c_all], [wuk_pad], _ep_head_rope(1.0, True), [(c_heads * C_HEAD_PAD, BF16, 256)],
                            name="l1_k", rows=mp,
                            aux=rope_aux(gk_pad) + [(kpe_all, "row")])
            (vh,) = _matmul([c_all], [c_w_uv[o].reshape(c_latent, c_heads * c_v).astype(BF16)], _ep_store,
                            [(c_heads * c_v, BF16, 256)], name="l1_vup", rows=mp)
            tq1 = 512
            nq1 = seq // tq1
            oc = _flash(qh, kh, vh, name="mla_prompt", grid=(batch, c_heads, nq1),
                        q_map=lambda b, h, i: (b * nq1 + i, h), k_map=lambda b, h, i: (b, h),
                        v_map=lambda b, h, i: (b, h), o_map=lambda b, h, i: (b * nq1 + i, h),
                        nhk=1, g=1, dq=C_HEAD_PAD, dv=c_v, tq=tq1, tk=512, skv=seq, causal=True,
                        out_rows=mp, out_cols=c_heads * c_v, out_dtype=BF16)
            wuk_t = jnp.transpose(c_w_uk[o], (1, 2, 0)).astype(BF16)
            gk2 = jnp.tile(c_gk[o][C_NOPE:].reshape(1, C_ROPE), (1, 2))
            qabs, qrot = _absorb(qh, wuk_t, c_gk[o][:C_NOPE].reshape(1, C_NOPE), gk2, mp=mp, db=db, t_new=t_new)
            rows_s = c_heads * t_new
            ckpe_t = jnp.transpose(cache_c_kpe, (0, 1, 3, 2))
            ctx = _mla_sample(qabs.reshape(db, rows_s, c_latent), qrot.reshape(db, rows_s, LANES),
                              wuk_t.reshape(c_heads * C_NOPE, c_latent), key_tab_t,
                              cache_c_latent, ckpe_t, o, page_table, c_all, kpe_all, new_tab,
                              mp=mp, t_new=t_new, pp=pp)
            wuv_h = jnp.transpose(c_w_uv[o], (1, 0, 2)).astype(BF16)
            oc_s = _mla_up(ctx.reshape(db, c_heads, t_new, c_latent), wuv_h)
            w_tri = jnp.tril(d_w_s[o])
            reps = D_CHUNK // t_new
            w_rep = jnp.einsum("ab,gij->gaibj", jnp.eye(reps, dtype=F32), w_tri[:, :t_new, :t_new])
            w_rep = w_rep.reshape(D_GROUPS, D_CHUNK, D_CHUNK)
            gd = d_width // D_GROUPS
            bias_p = jnp.repeat(d_b_s[o].T, gd, axis=1)
            bias_s = jnp.tile(bias_p[:t_new], (reps, 1))
            od = _chunk_gate(u, gv, w_tri.astype(BF16), bias_p, row0=0, rows=mp, name="gate_prompt")
            od_s = _chunk_gate(u, gv, w_rep.astype(BF16), bias_s, row0=mp, rows=ms, name="gate_sample")
            w_out = l1_w_out[o].astype(BF16)
            n_c = c_heads * c_v
            (x,) = _matmul([(oc, oc_s), (od, od_s)], [w_out[:n_c], w_out[n_c:]], _ep_residual, [(d, F32, 512)],
                           name="l1_out",
                           aux=[(x, "tile")], tn=512)
            cl_l.append((c_all[:mp].reshape(batch, seq, c_latent), c_all[mp:].reshape(db, t_new, c_latent)))
            ck_l.append((kpe_all[:mp, :C_ROPE].reshape(batch, seq, C_ROPE),
                         kpe_all[mp:, :C_ROPE].reshape(db, t_new, C_ROPE)))
            dv_l.append(gv[mp:].reshape(db, t_new, d_width))

        ones128 = _block_ones(x_width, X_DIM)
        gq_x = jnp.tile(x_gq[layer].reshape(1, X_DIM), (1, x_heads))
        gk_x = jnp.tile(x_gk[layer].reshape(1, X_DIM), (1, x_heads))
        (qx,) = _matmul([x], [x_wq[layer].astype(BF16)], _ep_segnorm(X_DIM, X_DIM ** -0.5), [(x_width, F32, x_width)],
                        name="x_q", gain=norm_mem[layer].reshape(1, d), tn=x_width,
                        aux=[(ones128, "const"), (gq_x, "const")])
        memf = mem_prompt.reshape(batch * mem_len, d)
        (mkp,) = _matmul([memf], [x_wk[layer].astype(BF16)], _ep_segnorm(X_DIM, 1.0), [(x_width, F32, x_width)],
                         name="x_k", tn=x_width, aux=[(ones128, "const"), (gk_x, "const")])
        (mvp,) = _matmul([memf], [x_wv[layer].astype(BF16)], _ep_store, [(x_width, F32, x_width)], name="x_v",
                         tn=x_width)
        ox = _flash(qx, mkp, mvp, name="cross_prompt", grid=(batch, 1, nq),
                    q_map=lambda b, h, i: (b * nq + i, 0), k_map=lambda b, h, i: (b, 0),
                    v_map=lambda b, h, i: (b, 0), o_map=lambda b, h, i: (b * nq + i, 0),
                    nhk=x_heads, g=1, dq=X_DIM, dv=X_DIM, tq=tq, tk=mem_len, skv=mem_len, causal=False,
                    out_rows=mp, out_cols=x_width, out_dtype=F32)
        cmk = cache_mem_k.reshape(depth * db * mem_len * x_heads, X_DIM)
        cmv = cache_mem_v.reshape(depth * db * mem_len * x_heads, X_DIM)
        ox_s = _cross_sample(qx, cmk, cmv, layer=layer, mp=mp, db=db, t_new=t_new, n_heads=x_heads, mem_len=mem_len)
        (x,) = _matmul([(ox, ox_s)], [x_wo[layer].astype(BF16)], _ep_residual, [(d, F32, 512)], name="x_out",
                       aux=[(x, "tile")], tn=512)
        mk_l.append(mkp.reshape(batch, mem_len, x_heads, X_DIM))
        mv_l.append(mvp.reshape(batch, mem_len, x_heads, X_DIM))

        x = _hmoe(x, norm_ffn[layer].reshape(1, d), moe_wg[layer], moe_bg[layer], moe_we[layer], moe_be[layer],
                  moe_w1, moe_w3, moe_w2, layer)

    stack = lambda items, k: jnp.stack([it[k] for it in items])
    return (x[:mp].reshape(batch, seq, d), x[mp:].reshape(db, t_new, d),
            stack(ak_l, 0), stack(av_l, 0), stack(pl_l, 0), stack(cl_l, 0), stack(ck_l, 0),
            jnp.stack(mk_l), jnp.stack(mv_l),
            stack(ak_l, 1), stack(av_l, 1), stack(pl_l, 1), stack(cl_l, 1), stack(ck_l, 1), jnp.stack(dv_l))
```

```python
import functools
import math

import jax
import jax.numpy as jnp
import numpy as np
from jax import lax
from jax.experimental import pallas as pl
from jax.experimental.pallas import tpu as pltpu

F32 = jnp.float32
BF16 = jnp.bfloat16
EPS = 1e-6
NEG_INF = -1e30
ROPE_THETA = 10000.0

LANES = 128
VMEM_LIMIT = 56 * 1024 * 1024

A_DIM = 64
A_KV_HEADS = 2
B_WINDOWS = (2, 4, 8, 16)
B_HIST = max(B_WINDOWS) - 1
C_NOPE = 128
C_ROPE = 64
C_QK = C_NOPE + C_ROPE
C_HEAD_PAD = 256
D_GROUPS = 4
D_CHUNK = 128
X_DIM = 128
TOP_K = 2
MOE_TILE = 256


def _params(*sem):
    return pltpu.CompilerParams(dimension_semantics=sem, vmem_limit_bytes=VMEM_LIMIT)


def _block_ones(n, seg):
    i = np.arange(n)
    return jnp.asarray((i[:, None] // seg) == (i[None, :] // seg), dtype=BF16)


def _rep_lanes(x, n):
    if n == LANES:
        return x
    if n < LANES:
        return x[:, :n]
    return jnp.concatenate([x] * (n // LANES), axis=1)


def _nt_dot(a, b):
    return lax.dot_general(a, b, (((1,), (1,)), ((), ())), preferred_element_type=F32)


def _mm_body(*refs, n_lhs, n_first, has_gain, n_aux, n_out, epilogue):
    per = 2 if n_first else 1
    xs = refs[:per * n_lhs]
    p = per * n_lhs
    gain = None
    if has_gain:
        gain = refs[p]
        p += 1
    ws = refs[p:p + n_lhs]
    p += n_lhs
    aux = refs[p:p + n_aux]
    p += n_aux
    outs = refs[p:p + n_out]
    lhs = refs[p + n_out:]

    @pl.when(pl.program_id(1) == 0)
    def _():
        for k in range(n_lhs):
            if n_first:
                x = jnp.where(pl.program_id(0) < n_first, xs[2 * k][...].astype(BF16),
                              xs[2 * k + 1][...].astype(BF16))
            else:
                x = xs[k][...]
            if gain is not None:
                xf = x.astype(F32)
                x = xf * lax.rsqrt(jnp.mean(xf * xf, axis=-1, keepdims=True) + EPS) * gain[...]
            lhs[k][...] = x.astype(BF16)

    acc = None
    for k, w_ref in enumerate(ws):
        y = jnp.dot(lhs[k][...], w_ref[...], preferred_element_type=F32)
        acc = y if acc is None else acc + y
    epilogue(acc, aux, outs)


def _matmul(xs, ws, epilogue, outs, *, name, gain=None, aux=(), tm=1024, tn=256, row0=0, rows=None):
    paired = isinstance(xs[0], tuple)
    n = ws[0].shape[1]
    n_first = 0
    if paired:
        assert row0 == 0 and rows is None and gain is None
        rows_first = xs[0][0].shape[0]
        rows = rows_first + xs[0][1].shape[0]
        tm = min(tm, rows_first, xs[0][1].shape[0])
        assert rows_first % tm == 0
        n_first = rows_first // tm
    else:
        rows = xs[0].shape[0] - row0 if rows is None else rows
        tm = min(tm, rows)
    assert rows % tm == 0 and row0 % tm == 0 and n % tn == 0
    r0 = row0 // tm
    n_j = n // tn
    grid = (rows // tm, n_j)
    in_specs, args = [], []
    for x in xs:
        if paired:
            in_specs += [pl.BlockSpec((tm, x[0].shape[1]), lambda i, j: (jnp.minimum(i, n_first - 1), 0)),
                         pl.BlockSpec((tm, x[1].shape[1]), lambda i, j: (jnp.maximum(i - n_first, 0), 0))]
            args += [x[0], x[1]]
        else:
            in_specs.append(pl.BlockSpec((tm, x.shape[1]), lambda i, j: (i + r0, 0)))
            args.append(x)
    if gain is not None:
        in_specs.append(pl.BlockSpec((1, gain.shape[1]), lambda i, j: (0, 0)))
        args.append(gain)
    in_specs += [pl.BlockSpec((w.shape[0], tn), lambda i, j: (0, j)) for w in ws]
    args += list(ws)
    for a, kind in aux:
        if kind == "const":
            spec = pl.BlockSpec(a.shape, lambda i, j: (0, 0))
        elif kind == "row":
            spec = pl.BlockSpec((tm, a.shape[1]), lambda i, j: (i + r0, 0))
        elif kind == "col":
            spec = pl.BlockSpec((1, a.shape[1] // n_j), lambda i, j: (0, j))
        else:
            spec = pl.BlockSpec((tm, tn), lambda i, j: (i + r0, j))
        in_specs.append(spec)
        args.append(a)
    out_shape = [jax.ShapeDtypeStruct((rows, c), dt) for (c, dt, _) in outs]
    out_specs = [pl.BlockSpec((tm, t), lambda i, j: (i, j)) for (_, _, t) in outs]
    body = functools.partial(_mm_body, n_lhs=len(xs), n_first=n_first, has_gain=gain is not None,
                             n_aux=len(aux), n_out=len(outs), epilogue=epilogue)
    scratch = [pltpu.VMEM((tm, w.shape[0]), BF16) for w in ws]
    return pl.pallas_call(body, grid=grid, in_specs=in_specs, out_specs=out_specs, out_shape=out_shape,
                          scratch_shapes=scratch,
                          compiler_params=_params("parallel", "arbitrary"), name=name)(*args)


def _ep_store(acc, aux, outs):
    outs[0][...] = acc.astype(outs[0].dtype)


def _ep_residual(acc, aux, outs):
    outs[0][...] = (aux[0][...] + acc).astype(outs[0].dtype)


def _seg_rms(y, ones_ref, gain_ref, seg):
    ssq = jnp.dot((y * y).astype(BF16), ones_ref[...], preferred_element_type=F32)
    return y * lax.rsqrt(ssq * (1.0 / seg) + EPS) * gain_ref[...]


def _ep_segnorm(seg, scale):
    def ep(acc, aux, outs):
        y = _seg_rms(acc, aux[0], aux[1], seg)
        if scale != 1.0:
            y = y * scale
        outs[0][...] = y.astype(outs[0].dtype)
    return ep


def _ep_diff_q(acc, aux, outs):
    y = _seg_rms(acc, aux[0], aux[1], A_DIM) * (A_DIM ** -0.5)
    lane = lax.broadcasted_iota(jnp.int32, (1, LANES), 1)
    lo = (lane < A_DIM).astype(F32)
    hi = 1.0 - lo
    o = outs[0]
    for h in range(2):
        blk = y[:, h * LANES:(h + 1) * LANES]
        o[:, (2 * h) * LANES:(2 * h + 1) * LANES] = (blk * lo).astype(o.dtype)
        o[:, (2 * h + 1) * LANES:(2 * h + 2) * LANES] = (blk * hi).astype(o.dtype)


def _ep_diff_kvb(acc, aux, outs):
    j = pl.program_id(1)

    @pl.when(j == 0)
    def _():
        outs[0][...] = _seg_rms(acc, aux[0], aux[1], A_DIM)

    @pl.when(j > 0)
    def _():
        outs[0][...] = acc


def _rope_half(b, cos, sin):
    lane = lax.broadcasted_iota(jnp.int32, b.shape, 1)
    half = C_ROPE // 2
    partner = jnp.where(lane < half, pltpu.roll(b, LANES - half, 1), pltpu.roll(b, half, 1))
    return b * cos + partner * sin


def _ep_head_rope(scale, with_kpe):
    def ep(acc, aux, outs):
        if with_kpe:
            g_ref, cos_ref, sin_ref, kpe_ref = aux
            y = jnp.concatenate([acc[:, :LANES], kpe_ref[...]], axis=1)
        else:
            g_ref, cos_ref, sin_ref = aux
            y = acc
        ms = jnp.sum(y * y, axis=-1, keepdims=True) * (1.0 / C_QK)
        yn = y * lax.rsqrt(ms + EPS) * g_ref[...]
        br = _rope_half(yn[:, LANES:], cos_ref[...], sin_ref[...])
        o = jnp.concatenate([yn[:, :LANES], br], axis=1)
        if scale != 1.0:
            o = o * scale
        outs[0][...] = o.astype(outs[0].dtype)
    return ep


def _ep_latent(c_latent):
    def ep(acc, aux, outs):
        c = acc[:, :c_latent]
        ms = jnp.mean(c * c, axis=-1, keepdims=True)
        outs[0][...] = c * lax.rsqrt(ms + EPS) * aux[0][...]
        outs[1][...] = acc[:, c_latent:]
    return ep


def _gelu(x):
    return 0.5 * x * (1.0 + jnp.tanh(math.sqrt(2.0 / math.pi) * (x + 0.044715 * (x * x * x))))


def _ep_gelu(acc, aux, outs):
    outs[0][...] = _gelu(acc)


def _ep_gelu_rms(acc, aux, outs):
    y = _gelu(acc)
    ms = jnp.mean(y * y, axis=-1, keepdims=True)
    outs[0][...] = y * lax.rsqrt(ms + EPS) * aux[0][...]


def _diff_lambda(lam_ref, lam_init):
    lp = lam_ref[...]
    a = jnp.sum(lp[0:1] * lp[1:2], axis=1, keepdims=True)
    b = jnp.sum(lp[2:3] * lp[3:4], axis=1, keepdims=True)
    return jnp.exp(a) - jnp.exp(b) + lam_init


def _diff_combine(o1, o2, lam, gout_ref, lam_init):
    d = o1 - lam * o2
    ms = jnp.mean(d * d, axis=-1, keepdims=True)
    return d * lax.rsqrt(ms + EPS) * gout_ref[...] * (1.0 - lam_init)


def _softmax_probs(s, m_s, l_s, h):
    m_prev = m_s[h]
    m_new = jnp.maximum(m_prev, jnp.max(s, axis=1, keepdims=True))
    alpha = jnp.exp(m_prev - m_new)
    p = jnp.exp(s - _rep_lanes(m_new, s.shape[1]))
    l_s[h] = alpha * l_s[h] + jnp.sum(p, axis=1, keepdims=True)
    m_s[h] = m_new
    return p, alpha


def _softmax_step(s, vt, m_s, l_s, acc_s, h):
    p, alpha = _softmax_probs(s, m_s, l_s, h)
    pv = jnp.dot(p.astype(vt.dtype), vt, preferred_element_type=F32)
    acc_s[h] = _rep_lanes(alpha, pv.shape[1]) * acc_s[h] + pv


def _flash_body(*refs, nhk, g, dq, dv, tq, tk, skv, causal, diff, lam_init):
    if diff:
        q_ref, k_ref, v_ref, lam_ref, gout_ref, o_ref, m_s, l_s, acc_s = refs
    else:
        q_ref, k_ref, v_ref, o_ref, m_s, l_s, acc_s = refs
    qi = pl.program_id(2)
    m_s[...] = jnp.full(m_s.shape, NEG_INF, F32)
    l_s[...] = jnp.zeros(l_s.shape, F32)
    acc_s[...] = jnp.zeros(acc_s.shape, F32)

    def step(ki, masked):
        k0 = pl.multiple_of(ki * tk, tk)
        if masked:
            rows = qi * tq + lax.broadcasted_iota(jnp.int32, (tq, tk), 0)
            cols = k0 + lax.broadcasted_iota(jnp.int32, (tq, tk), 1)
            keep = rows >= cols
        for hk in range(nhk):
            kt = k_ref[pl.ds(k0, tk), hk * dq:(hk + 1) * dq].astype(BF16)
            vt = v_ref[pl.ds(k0, tk), hk * dv:(hk + 1) * dv].astype(BF16)
            for gi in range(g):
                h = hk * g + gi
                s = _nt_dot(q_ref[:, h * dq:(h + 1) * dq].astype(BF16), kt)
                if masked:
                    s = jnp.where(keep, s, NEG_INF)
                _softmax_step(s, vt, m_s, l_s, acc_s, h)

    def loop(lo, hi, masked):
        def body(ki, c):
            step(ki, masked)
            return c
        lax.fori_loop(lo, hi, body, 0)

    if causal:
        n_full = (qi * tq) // tk
        n_all = (qi * tq + tq + tk - 1) // tk
        loop(0, n_full, False)
        loop(n_full, n_all, True)
    elif skv == tk:
        step(0, False)
    else:
        loop(0, skv // tk, False)

    nh = nhk * g
    if diff:
        lam = _diff_lambda(lam_ref, lam_init)
        for j in range(nh // 2):
            o1 = acc_s[2 * j] / l_s[2 * j]
            o2 = acc_s[2 * j + 1] / l_s[2 * j + 1]
            o_ref[:, j * dv:(j + 1) * dv] = _diff_combine(o1, o2, lam, gout_ref, lam_init).astype(o_ref.dtype)
    else:
        for h in range(nh):
            o_ref[:, h * dv:(h + 1) * dv] = (acc_s[h] / l_s[h]).astype(o_ref.dtype)


def _flash(q, k, v, *, name, grid, q_map, k_map, v_map, o_map, nhk, g, dq, dv, tq, tk, skv, causal,
           out_rows, out_cols, out_dtype, diff=None):
    assert dv == LANES
    nh = nhk * g
    n_out_heads = nh // 2 if diff else nh
    in_specs = [pl.BlockSpec((tq, nh * dq), q_map),
                pl.BlockSpec((skv, nhk * dq), k_map),
                pl.BlockSpec((skv, nhk * dv), v_map)]
    args = [q, k, v]
    lam_init = 0.0
    if diff:
        lam_p, gout, lam_init = diff
        in_specs += [pl.BlockSpec(lam_p.shape, lambda b, h, i: (0, 0)),
                     pl.BlockSpec(gout.shape, lambda b, h, i: (0, 0))]
        args += [lam_p, gout]
    body = functools.partial(_flash_body, nhk=nhk, g=g, dq=dq, dv=dv, tq=tq, tk=tk, skv=skv, causal=causal,
                             diff=bool(diff), lam_init=lam_init)
    return pl.pallas_call(
        body, grid=grid, in_specs=in_specs,
        out_specs=pl.BlockSpec((tq, n_out_heads * dv), o_map),
        out_shape=jax.ShapeDtypeStruct((out_rows, out_cols), out_dtype),
        scratch_shapes=[pltpu.VMEM((nh, tq, LANES), F32), pltpu.VMEM((nh, tq, LANES), F32),
                        pltpu.VMEM((nh, tq, dv), F32)],
        compiler_params=_params("parallel", "parallel", "arbitrary"), name=name)(*args)


def _cross_sample_body(q_ref, k_ref, v_ref, o_ref, *, bb, t_new, n_heads, mem_len):
    for i in range(bb):
        for h in range(n_heads):
            r0 = i * mem_len * n_heads + h
            kt = k_ref[pl.ds(r0, mem_len, stride=n_heads), :].astype(BF16)
            vt = v_ref[pl.ds(r0, mem_len, stride=n_heads), :].astype(BF16)
            q = q_ref[pl.ds(i * t_new, t_new), h * X_DIM:(h + 1) * X_DIM].astype(BF16)
            s = _nt_dot(q, kt)
            p = jnp.exp(s - jnp.max(s, axis=1, keepdims=True))
            o = jnp.dot(p.astype(BF16), vt, preferred_element_type=F32) / jnp.sum(p, axis=1, keepdims=True)
            o_ref[pl.ds(i * t_new, t_new), h * X_DIM:(h + 1) * X_DIM] = o


def _cross_sample(qx, mem_k, mem_v, *, layer, mp, db, t_new, n_heads, mem_len, bb=4):
    width = n_heads * X_DIM
    kv_rows = bb * mem_len * n_heads
    r0 = mp // (bb * t_new)
    k0 = layer * (db // bb)
    body = functools.partial(_cross_sample_body, bb=bb, t_new=t_new, n_heads=n_heads, mem_len=mem_len)
    return pl.pallas_call(
        body, grid=(db // bb,),
        in_specs=[pl.BlockSpec((bb * t_new, width), lambda i: (r0 + i, 0)),
                  pl.BlockSpec((kv_rows, X_DIM), lambda i: (k0 + i, 0)),
                  pl.BlockSpec((kv_rows, X_DIM), lambda i: (k0 + i, 0))],
        out_specs=pl.BlockSpec((bb * t_new, width), lambda i: (i, 0)),
        out_shape=jax.ShapeDtypeStruct((db * t_new, width), F32),
        compiler_params=_params("parallel"), name="cross_sample")(qx, mem_k, mem_v)


def _page_copies(caches, dsts, pt_ref, layer, b, c, slot, sems, pp):
    out = []
    for p in range(pp):
        pg = pt_ref[b, c * pp + p]
        for cache, dst in zip(caches, dsts):
            out.append(pltpu.make_async_copy(cache.at[layer, pg], dst(slot, p), sems.at[slot]))
    return out


def _rows_window(buf, n):
    return lambda slot, p: buf.at[slot, pl.ds(p * n, n)]


def _lanes_window(buf, n):
    return lambda slot, p: buf.at[slot, :, pl.ds(p * n, n)]


def _paged_loop(caches, dsts, pt_ref, layer, sems, pp, n_chunks, consume):
    b = pl.program_id(0)
    nb = pl.num_programs(0)

    def start(bb, cc, slot):
        for cp in _page_copies(caches, dsts, pt_ref, layer, bb, cc, slot, sems, pp):
            cp.start()

    @pl.when(b == 0)
    def _():
        start(0, 0, 0)

    def chunk(c, carry):
        slot = (b * n_chunks + c) % 2
        last = c == n_chunks - 1

        @pl.when(jnp.logical_or(jnp.logical_not(last), b + 1 < nb))
        def _():
            start(jnp.where(last, b + 1, b), jnp.where(last, 0, c + 1), 1 - slot)

        for cp in _page_copies(caches, dsts, pt_ref, layer, b, c, slot, sems, pp):
            cp.wait()
        consume(slot, c)
        return carry

    lax.fori_loop(0, n_chunks, chunk, 0)


def _diff_sample_body(pt_ref, q_ref, ck_hbm, cv_hbm, knew_ref, vnew_ref, lam_ref, gout_ref, o_ref,
                      kbuf, vbuf, sems, m_s, l_s, acc_s, *, layer, pp, page, n_chunks, t_new, lam_init):
    m_s[...] = jnp.full(m_s.shape, NEG_INF, F32)
    l_s[...] = jnp.zeros(l_s.shape, F32)
    acc_s[...] = jnp.zeros(acc_s.shape, F32)
    q = q_ref[0]
    rows = q.shape[0]
    half = rows // A_KV_HEADS
    n_pos = pp * page

    def consume(slot, c):
        kt = jnp.concatenate([kbuf[slot, i].astype(BF16) for i in range(pp)], axis=1)
        s = jnp.dot(q, kt, preferred_element_type=F32)
        p, alpha = _softmax_probs(s, m_s, l_s, 0)
        pb = p.astype(BF16)
        pv = [jnp.dot(pb[kh * half:(kh + 1) * half],
                      vbuf[slot, pl.ds(kh, n_pos, stride=A_KV_HEADS), :].astype(BF16),
                      preferred_element_type=F32) for kh in range(A_KV_HEADS)]
        acc_s[0] = alpha * acc_s[0] + jnp.concatenate(pv, axis=0)

    _paged_loop((ck_hbm, cv_hbm), (lambda slot, i: kbuf.at[slot, i], _rows_window(vbuf, A_KV_HEADS * page)),
                pt_ref, layer, sems, pp, n_chunks, consume)

    s = _nt_dot(q.astype(F32), knew_ref[...])
    t_row = lax.broadcasted_iota(jnp.int32, (rows, t_new), 0) % t_new
    l_col = lax.broadcasted_iota(jnp.int32, (rows, t_new), 1)
    s = jnp.where(t_row >= l_col, s, NEG_INF)
    p, alpha = _softmax_probs(s, m_s, l_s, 0)
    vnew = vnew_ref[...]
    pv = [jnp.dot(p[kh * half:(kh + 1) * half], vnew[:, kh * LANES:(kh + 1) * LANES], preferred_element_type=F32)
          for kh in range(A_KV_HEADS)]
    o = (alpha * acc_s[0] + jnp.concatenate(pv, axis=0)) / l_s[0]
    lam = _diff_lambda(lam_ref, lam_init)
    grp = half // 2 // t_new
    for kh in range(A_KV_HEADS):
        ok = o[kh * half:(kh + 1) * half]
        d = _diff_combine(ok[:half // 2], ok[half // 2:], lam, gout_ref, lam_init)
        for gi in range(grp):
            c0 = (kh * grp + gi) * LANES
            o_ref[:, c0:c0 + LANES] = d[gi * t_new:(gi + 1) * t_new].astype(o_ref.dtype)


def _diff_sample(qbd, cache_kt, cache_v, e, page_table, kvb, lam_p, gout, lam_init, *, mp, t_new, pp):
    db, n_pages = page_table.shape
    kw, page = cache_kt.shape[2], cache_kt.shape[3]
    dv = cache_v.shape[3]
    rows = qbd.shape[1]
    assert n_pages % pp == 0 and dv == LANES
    r0 = mp // t_new
    out_w = rows // 2 * LANES // t_new
    in_specs = [pl.BlockSpec((1, rows, kw), lambda b, pt: (b, 0, 0)),
                pl.BlockSpec(memory_space=pl.ANY), pl.BlockSpec(memory_space=pl.ANY),
                pl.BlockSpec((t_new, kw), lambda b, pt: (r0 + b, 0)),
                pl.BlockSpec((t_new, kw), lambda b, pt: (r0 + b, 1)),
                pl.BlockSpec(lam_p.shape, lambda b, pt: (0, 0)),
                pl.BlockSpec(gout.shape, lambda b, pt: (0, 0))]
    gs = pltpu.PrefetchScalarGridSpec(
        num_scalar_prefetch=1, grid=(db,), in_specs=in_specs,
        out_specs=pl.BlockSpec((t_new, out_w), lambda b, pt: (b, 0)),
        scratch_shapes=[pltpu.VMEM((2, pp, kw, page), F32), pltpu.VMEM((2, pp * page * A_KV_HEADS, dv), F32),
                        pltpu.SemaphoreType.DMA((2,)),
                        pltpu.VMEM((1, rows, LANES), F32), pltpu.VMEM((1, rows, LANES), F32),
                        pltpu.VMEM((1, rows, dv), F32)])
    body = functools.partial(_diff_sample_body, layer=e, pp=pp, page=page, n_chunks=n_pages // pp,
                             t_new=t_new, lam_init=lam_init)
    return pl.pallas_call(body, grid_spec=gs, out_shape=jax.ShapeDtypeStruct((db * t_new, out_w), F32),
                          compiler_params=_params("arbitrary"), name="diff_sample")(
                              page_table, qbd, cache_kt, cache_v, kvb, kvb, lam_p, gout)


def _pool_finish(d, g, w_ref, scale_ref, gd):
    y = jnp.dot(d.astype(BF16), w_ref[g], preferred_element_type=F32)
    return y * scale_ref[:, g * gd:(g + 1) * gd]


def _pool_prompt_body(*refs, tm, gd, n_grp):
    cur = refs[:n_grp]
    prev = refs[n_grp:2 * n_grp]
    w_ref, scale_ref, o_ref, ext = refs[2 * n_grp:]
    i = pl.program_id(1)
    hist = 16
    pos = i * tm + lax.broadcasted_iota(jnp.int32, (tm, 1), 0)
    for g, w in enumerate(B_WINDOWS):
        x = cur[g][...]
        ext[pl.ds(hist, tm), :] = x
        ext[pl.ds(0, hist), :] = jnp.where(i > 0, prev[g][...], 0.0)
        s = x
        for k in range(1, w):
            s = s + ext[pl.ds(hist - k, tm), :]
        cnt = jnp.minimum(pos + 1, w).astype(F32)
        d = s / cnt - x
        o_ref[:, g * gd:(g + 1) * gd] = _pool_finish(d, g, w_ref, scale_ref, gd).astype(o_ref.dtype)


def _pool_prompt(kvb, col0, w_lin, scale, *, batch, seq, rows_out, tm=512):
    n_grp, gd, _ = w_lin.shape
    nt = seq // tm
    c0 = col0 // gd
    hist = 16
    in_specs = [pl.BlockSpec((tm, gd), functools.partial(lambda b, i, g: (b * nt + i, c0 + g), g=g))
                for g in range(n_grp)]
    in_specs += [pl.BlockSpec((hist, gd), functools.partial(
        lambda b, i, g: (jnp.maximum((b * nt + i) * (tm // hist) - 1, 0), c0 + g), g=g)) for g in range(n_grp)]
    in_specs += [pl.BlockSpec(w_lin.shape, lambda b, i: (0, 0, 0)),
                 pl.BlockSpec(scale.shape, lambda b, i: (0, 0))]
    body = functools.partial(_pool_prompt_body, tm=tm, gd=gd, n_grp=n_grp)
    return pl.pallas_call(
        body, grid=(batch, nt), in_specs=in_specs,
        out_specs=pl.BlockSpec((tm, n_grp * gd), lambda b, i: (b * nt + i, 0)),
        out_shape=jax.ShapeDtypeStruct((rows_out, n_grp * gd), BF16),
        scratch_shapes=[pltpu.VMEM((tm + hist, gd), F32)],
        compiler_params=_params("parallel", "arbitrary"), name="pool_prompt")(
            *([kvb] * (2 * n_grp)), w_lin, scale)


def _pool_sample_body(ext_ref, w_ref, scale_ref, o_ref, *, bb, t_new, gd, past):
    pos = past + lax.broadcasted_iota(jnp.int32, (1, t_new, 1), 1)
    for g, w in enumerate(B_WINDOWS):
        x = ext_ref[:, pl.ds(B_HIST, t_new), pl.ds(g * gd, gd)]
        s = x
        for k in range(1, w):
            s = s + ext_ref[:, pl.ds(B_HIST - k, t_new), pl.ds(g * gd, gd)]
        cnt = jnp.minimum(pos + 1, w).astype(F32)
        d = (s / cnt - x).reshape(bb * t_new, gd)
        o_ref[:, g * gd:(g + 1) * gd] = _pool_finish(d, g, w_ref, scale_ref, gd).astype(o_ref.dtype)


def _pool_sample(ext, w_lin, scale, *, past, bb=16):
    db, n_ext, width = ext.shape
    t_new = n_ext - B_HIST
    n_grp, gd, _ = w_lin.shape
    body = functools.partial(_pool_sample_body, bb=bb, t_new=t_new, gd=gd, past=past)
    return pl.pallas_call(
        body, grid=(db // bb,),
        in_specs=[pl.BlockSpec((bb, n_ext, width), lambda i: (i, 0, 0)),
                  pl.BlockSpec(w_lin.shape, lambda i: (0, 0, 0)),
                  pl.BlockSpec(scale.shape, lambda i: (0, 0))],
        out_specs=pl.BlockSpec((bb * t_new, width), lambda i: (i, 0)),
        out_shape=jax.ShapeDtypeStruct((db * t_new, width), BF16),
        compiler_params=_params("parallel"), name="pool_sample")(ext, w_lin, scale)


def _chunk_gate_body(u_ref, v_ref, w_ref, b_ref, o_ref, *, gd, n_grp):
    for g in range(n_grp):
        cs = slice(g * gd, (g + 1) * gd)
        mix = jnp.dot(w_ref[g], v_ref[:, cs].astype(BF16), preferred_element_type=F32) + b_ref[:, cs]
        o_ref[:, cs] = (u_ref[:, cs] * mix).astype(o_ref.dtype)


def _chunk_gate(u, v, wmix, bias, *, row0, rows, name):
    width = u.shape[1]
    n_grp, r, _ = wmix.shape
    gd = width // n_grp
    r0 = row0 // r
    body = functools.partial(_chunk_gate_body, gd=gd, n_grp=n_grp)
    return pl.pallas_call(
        body, grid=(rows // r,),
        in_specs=[pl.BlockSpec((r, width), lambda i: (r0 + i, 0)),
                  pl.BlockSpec((r, width), lambda i: (r0 + i, 0)),
                  pl.BlockSpec(wmix.shape, lambda i: (0, 0, 0)),
                  pl.BlockSpec(bias.shape, lambda i: (0, 0))],
        out_specs=pl.BlockSpec((r, width), lambda i: (i, 0)),
        out_shape=jax.ShapeDtypeStruct((rows, width), BF16),
        compiler_params=_params("parallel"), name=name)(u, v, wmix, bias)


def _absorb_body(q_ref, w_ref, gk_ref, gk2_ref, qabs_ref, qrot_ref, *, tb, t_new):
    q = q_ref[...].astype(F32)
    qn = (q[:, :C_NOPE] * gk_ref[...]).astype(BF16)
    y = jnp.dot(qn, w_ref[0], preferred_element_type=F32)
    qabs_ref[...] = y.reshape(tb, 1, t_new, y.shape[1])
    pe = q[:, C_NOPE:]
    lane = lax.broadcasted_iota(jnp.int32, pe.shape, 1)
    half = C_ROPE // 2
    swapped = jnp.where(lane < half, pltpu.roll(pe, LANES - half, 1), -pltpu.roll(pe, half, 1))
    swapped = jnp.where(lane < C_ROPE, swapped, 0.0)
    qrot = (pe + pltpu.roll(swapped, C_ROPE, 1)) * gk2_ref[...]
    qrot_ref[...] = qrot.reshape(tb, 1, t_new, LANES)


def _absorb(qh, wuk_t, gk_nope, gk_rope2, *, mp, db, t_new, tb=16):
    n_heads, _, c_latent = wuk_t.shape
    tm = tb * t_new
    r0 = mp // tm
    body = functools.partial(_absorb_body, tb=tb, t_new=t_new)
    shp = lambda w: jax.ShapeDtypeStruct((db, n_heads, t_new, w), F32)
    spec = lambda w: pl.BlockSpec((tb, 1, t_new, w), lambda i, h: (i, h, 0, 0))
    return pl.pallas_call(
        body, grid=(db // tb, n_heads),
        in_specs=[pl.BlockSpec((tm, C_HEAD_PAD), lambda i, h: (r0 + i, h)),
                  pl.BlockSpec((1, C_NOPE, c_latent), lambda i, h: (h, 0, 0)),
                  pl.BlockSpec((1, C_NOPE), lambda i, h: (0, 0)),
                  pl.BlockSpec((1, LANES), lambda i, h: (0, 0))],
        out_specs=[spec(c_latent), spec(LANES)],
        out_shape=[shp(c_latent), shp(LANES)],
        compiler_params=_params("parallel", "arbitrary"), name="mla_absorb")(qh, wuk_t, gk_nope, gk_rope2)


def _mla_sample_body(pt_ref, qabs_ref, qrot_ref, wuk_ref, tabt_ref, cc_hbm, ckt_hbm, cnew_ref, knew_ref,
                     tabn_ref, o_ref, cbuf, kbuf, sems, lhs_s, cb_s, s_s, m_s, l_s, acc_s, *,
                     layer, pp, page, n_chunks, tl, t_new, n_heads):
    b = pl.program_id(0)
    n_up = wuk_ref.shape[0]
    hd = n_up // n_heads
    n_pos = pp * page
    c_latent = cbuf.shape[2]

    @pl.when(b == 0)
    def _():
        lhs_s[pl.ds(0, n_up), :] = wuk_ref[...]

    lhs_s[pl.ds(n_up, n_heads * t_new), :] = qabs_ref[0].astype(BF16)
    m_s[...] = jnp.full(m_s.shape, NEG_INF, F32)
    l_s[...] = jnp.zeros(l_s.shape, F32)
    acc_s[...] = jnp.zeros(acc_s.shape, F32)
    qrot = qrot_ref[0]
    qrot_b = qrot.astype(BF16)

    def scores(big, ksq, s_pe):
        lk = big.shape[1]
        kn = big[:n_up]
        nsq = jnp.sum((kn * kn).reshape(n_heads, hd, lk), axis=1)
        r = lax.rsqrt((nsq + ksq) * (1.0 / C_QK) + EPS)
        r_rows = jnp.concatenate([jnp.broadcast_to(r[h:h + 1], (t_new, lk)) for h in range(n_heads)], axis=0)
        return (big[n_up:] + s_pe) * r_rows

    def consume(slot, c):
        lhs = lhs_s[...]
        for t in range(n_pos // tl):
            cb = cbuf[slot, pl.ds(t * tl, tl), :].astype(BF16)
            cb_s[pl.ds(t * tl, tl), :] = cb
            kt = kbuf[slot, :, pl.ds(t * tl, tl)]
            l0 = pl.multiple_of(c * n_pos + t * tl, tl)
            krot = jnp.concatenate([kt, kt], axis=0) * tabt_ref[:, pl.ds(l0, tl)]
            s_pe = jnp.dot(qrot_b, krot.astype(BF16), preferred_element_type=F32)
            ksq = jnp.sum(kt * kt, axis=0, keepdims=True)
            s_s[:, pl.ds(t * tl, tl)] = scores(_nt_dot(lhs, cb), ksq, s_pe)
        p, alpha = _softmax_probs(s_s[...], m_s, l_s, 0)
        pv = jnp.dot(p.astype(BF16), cb_s[...], preferred_element_type=F32)
        acc_s[0] = _rep_lanes(alpha, c_latent) * acc_s[0] + pv

    _paged_loop((cc_hbm, ckt_hbm), (_rows_window(cbuf, page), _lanes_window(kbuf, page)),
                pt_ref, layer, sems, pp, n_chunks, consume)

    rows = n_heads * t_new
    cn = cnew_ref[...]
    kpe = knew_ref[:, :C_ROPE]
    krot = jnp.concatenate([kpe, kpe], axis=1) * tabn_ref[...]
    ksq = _nt_dot(jnp.ones((8, C_ROPE), F32), kpe * kpe)[:1]
    s = scores(_nt_dot(lhs_s[...].astype(F32), cn), ksq, _nt_dot(qrot, krot))
    t_row = lax.broadcasted_iota(jnp.int32, (rows, t_new), 0) % t_new
    l_col = lax.broadcasted_iota(jnp.int32, (rows, t_new), 1)
    s = jnp.where(t_row >= l_col, s, NEG_INF)
    p, alpha = _softmax_probs(s, m_s, l_s, 0)
    pv = jnp.dot(p, cn, preferred_element_type=F32)
    o_ref[0] = (_rep_lanes(alpha, c_latent) * acc_s[0] + pv) / _rep_lanes(l_s[0], c_latent)


def _mla_sample(qabs, qrot, wuk2, tab_t, cache_c, cache_kpe_t, o, page_table, c_all, kpe_all, tab_new, *,
                mp, t_new, pp, tl=256):
    db, n_pages = page_table.shape
    page, c_latent = cache_c.shape[2], cache_c.shape[3]
    rows = qabs.shape[1]
    n_heads = rows // t_new
    n_up = wuk2.shape[0]
    n_pos = pp * page
    assert n_pages % pp == 0 and n_pos % tl == 0
    r0 = mp // t_new
    in_specs = [pl.BlockSpec((1, rows, c_latent), lambda b, pt: (b, 0, 0)),
                pl.BlockSpec((1, rows, LANES), lambda b, pt: (b, 0, 0)),
                pl.BlockSpec(wuk2.shape, lambda b, pt: (0, 0)),
                pl.BlockSpec(tab_t.shape, lambda b, pt: (0, 0)),
                pl.BlockSpec(memory_space=pl.ANY), pl.BlockSpec(memory_space=pl.ANY),
                pl.BlockSpec((t_new, c_latent), lambda b, pt: (r0 + b, 0)),
                pl.BlockSpec((t_new, LANES), lambda b, pt: (r0 + b, 0)),
                pl.BlockSpec(tab_new.shape, lambda b, pt: (0, 0))]
    gs = pltpu.PrefetchScalarGridSpec(
        num_scalar_prefetch=1, grid=(db,), in_specs=in_specs,
        out_specs=pl.BlockSpec((1, rows, c_latent), lambda b, pt: (b, 0, 0)),
        scratch_shapes=[pltpu.VMEM((2, n_pos, c_latent), F32), pltpu.VMEM((2, C_ROPE, n_pos), F32),
                        pltpu.SemaphoreType.DMA((2,)),
                        pltpu.VMEM((n_up + rows, c_latent), BF16),
                        pltpu.VMEM((n_pos, c_latent), BF16), pltpu.VMEM((rows, n_pos), F32),
                        pltpu.VMEM((1, rows, LANES), F32), pltpu.VMEM((1, rows, LANES), F32),
                        pltpu.VMEM((1, rows, c_latent), F32)])
    body = functools.partial(_mla_sample_body, layer=o, pp=pp, page=page, n_chunks=n_pages // pp, tl=tl,
                             t_new=t_new, n_heads=n_heads)
    return pl.pallas_call(body, grid_spec=gs, out_shape=jax.ShapeDtypeStruct((db, rows, c_latent), F32),
                          compiler_params=_params("arbitrary"), name="mla_sample")(
                              page_table, qabs, qrot, wuk2, tab_t, cache_c, cache_kpe_t, c_all, kpe_all, tab_new)


def _mla_up_body(ctx_ref, w_ref, o_ref, *, tb, t_new):
    c = ctx_ref[...].reshape(tb * t_new, ctx_ref.shape[3]).astype(BF16)
    o_ref[...] = jnp.dot(c, w_ref[0], preferred_element_type=F32).astype(o_ref.dtype)


def _mla_up(ctx4, wuv_h, *, tb=16):
    db, n_heads, t_new, c_latent = ctx4.shape
    dv = wuv_h.shape[2]
    body = functools.partial(_mla_up_body, tb=tb, t_new=t_new)
    return pl.pallas_call(
        body, grid=(db // tb, n_heads),
        in_specs=[pl.BlockSpec((tb, 1, t_new, c_latent), lambda i, h: (i, h, 0, 0)),
                  pl.BlockSpec((1, c_latent, dv), lambda i, h: (h, 0, 0))],
        out_specs=pl.BlockSpec((tb * t_new, dv), lambda i, h: (i, h)),
        out_shape=jax.ShapeDtypeStruct((db * t_new, n_heads * dv), BF16),
        compiler_params=_params("parallel", "arbitrary"), name="mla_up")(ctx4, wuv_h)


def _router_body(x_ref, g_ref, w_ref, b_ref, h_ref, r_ref, *, n_groups, per_group):
    xf = x_ref[...]
    h = xf * lax.rsqrt(jnp.mean(xf * xf, axis=-1, keepdims=True) + EPS) * g_ref[...]
    h_ref[...] = h
    hh = h.astype(BF16)
    hl = (h - hh.astype(F32)).astype(BF16)
    w = w_ref[...]
    y1 = jnp.dot(hh, w, preferred_element_type=F32)
    y2 = jnp.dot(hl, w[:, :LANES], preferred_element_type=F32)
    logits = y1[:, :LANES] + y1[:, LANES:] + y2 + b_ref[...]
    lane = lax.broadcasted_iota(jnp.int32, logits.shape, 1)
    big = jnp.int32(1 << 20)
    low = -3.0e38
    n_e = n_groups * per_group
    is_g = lane < n_groups
    gl = jnp.where(is_g, logits, low)
    gmax = jnp.max(gl, axis=1, keepdims=True)
    gsel = jnp.min(jnp.where(gl == gmax, lane, big), axis=1, keepdims=True)
    gsum = jnp.sum(jnp.where(is_g, jnp.exp(gl - gmax), 0.0), axis=1, keepdims=True)
    gprob = 1.0 / gsum
    e_lane = lane - n_groups
    e_grp = jnp.right_shift(e_lane, per_group.bit_length() - 1)
    in_grp = jnp.logical_and(jnp.logical_and(e_lane >= 0, e_lane < n_e), e_grp == gsel)
    el = jnp.where(in_grp, logits, low)
    t1 = jnp.max(el, axis=1, keepdims=True)
    i1 = jnp.min(jnp.where(el == t1, lane, big), axis=1, keepdims=True)
    el2 = jnp.where(lane == i1, low, el)
    t2 = jnp.max(el2, axis=1, keepdims=True)
    i2 = jnp.min(jnp.where(el2 == t2, lane, big), axis=1, keepdims=True)
    ex = jnp.exp(t2 - t1)
    w1 = gprob / (1.0 + ex)
    w2 = gprob * ex / (1.0 + ex)
    out = jnp.where(lane == 0, (i1 - n_groups).astype(F32), 0.0)
    out = jnp.where(lane == 1, (i2 - n_groups).astype(F32), out)
    out = jnp.where(lane == 2, w1, out)
    out = jnp.where(lane == 3, w2, out)
    r_ref[...] = out


def _router(x, gain, wcat, bias, *, n_groups, per_group, tm=512):
    rows, d = x.shape
    body = functools.partial(_router_body, n_groups=n_groups, per_group=per_group)
    return pl.pallas_call(
        body, grid=(rows // tm,),
        in_specs=[pl.BlockSpec((tm, d), lambda i: (i, 0)), pl.BlockSpec((1, d), lambda i: (0, 0)),
                  pl.BlockSpec(wcat.shape, lambda i: (0, 0)), pl.BlockSpec(bias.shape, lambda i: (0, 0))],
        out_specs=[pl.BlockSpec((tm, d), lambda i: (i, 0)), pl.BlockSpec((tm, LANES), lambda i: (i, 0))],
        out_shape=[jax.ShapeDtypeStruct((rows, d), F32), jax.ShapeDtypeStruct((rows, LANES), F32)],
        compiler_params=_params("parallel"), name="moe_router")(x, gain, wcat, bias)


def _gather_copy(src_hbm, idx_ref, base, r, dst, sem):
    return pltpu.make_async_copy(src_hbm.at[pl.ds(idx_ref[base + r], 1)], dst.at[pl.ds(r, 1)], sem)


def _start_gather(src_hbm, idx_ref, base, n, dst, sem):
    for r in range(n):
        _gather_copy(src_hbm, idx_ref, base, r, dst, sem).start()


def _wait_gather(src_hbm, idx_ref, base, n, dst, sem):
    for r in range(n):
        _gather_copy(src_hbm, idx_ref, base, r, dst, sem).wait()


def _experts_body(te_ref, nt_ref, tok_ref, h_hbm, w1_ref, w3_ref, w2_ref, o_ref, xbuf, sems, w1b, w3b, w2b, *, tm):
    t = pl.program_id(0)
    nt = nt_ref[0]
    slot = t % 2

    @pl.when(jnp.logical_and(t == 0, nt > 0))
    def _():
        _start_gather(h_hbm, tok_ref, 0, tm, xbuf.at[0], sems.at[0])

    @pl.when(t + 1 < nt)
    def _():
        _start_gather(h_hbm, tok_ref, (t + 1) * tm, tm, xbuf.at[1 - slot], sems.at[1 - slot])

    @pl.when(t < nt)
    def _():
        changed = jnp.logical_or(t == 0, te_ref[t] != te_ref[jnp.maximum(t - 1, 0)])

        @pl.when(changed)
        def _():
            w1b[...] = w1_ref[0, 0].astype(BF16)
            w3b[...] = w3_ref[0, 0].astype(BF16)
            w2b[...] = w2_ref[0, 0].astype(BF16)

        _wait_gather(h_hbm, tok_ref, t * tm, tm, xbuf.at[slot], sems.at[slot])
        x = xbuf[slot].astype(BF16)
        a = jnp.dot(x, w1b[...], preferred_element_type=F32)
        b = jnp.dot(x, w3b[...], preferred_element_type=F32)
        hid = (a * jax.nn.sigmoid(a)) * b
        o_ref[...] = jnp.dot(hid.astype(BF16), w2b[...], preferred_element_type=F32)

    @pl.when(t >= nt)
    def _():
        o_ref[...] = jnp.zeros(o_ref.shape, o_ref.dtype)


def _experts(h, tile_e, n_tiles, slot_tok, w1, w3, w2, *, layer, tm):
    n_slots = slot_tok.shape[0]
    t_max = n_slots // tm
    d = h.shape[1]
    f = w1.shape[3]
    gs = pltpu.PrefetchScalarGridSpec(
        num_scalar_prefetch=3, grid=(t_max,),
        in_specs=[pl.BlockSpec(memory_space=pl.ANY),
                  pl.BlockSpec((1, 1, d, f), lambda t, te, nt, tok: (layer, te[t], 0, 0)),
                  pl.BlockSpec((1, 1, d, f), lambda t, te, nt, tok: (layer, te[t], 0, 0)),
                  pl.BlockSpec((1, 1, f, d), lambda t, te, nt, tok: (layer, te[t], 0, 0))],
        out_specs=pl.BlockSpec((tm, d), lambda t, te, nt, tok: (t, 0)),
        scratch_shapes=[pltpu.VMEM((2, tm, d), F32), pltpu.SemaphoreType.DMA((2,)),
                        pltpu.VMEM((d, f), BF16), pltpu.VMEM((d, f), BF16), pltpu.VMEM((f, d), BF16)])
    body = functools.partial(_experts_body, tm=tm)
    return pl.pallas_call(body, grid_spec=gs, out_shape=jax.ShapeDtypeStruct((n_slots, d), F32),
                          compiler_params=_params("arbitrary"), name="moe_experts")(
                              tile_e, n_tiles, slot_tok, h, w1, w3, w2)


def _combine_body(slot_ref, x_ref, r_ref, y_hbm, o_ref, buf, sems, *, tm, top_k):
    i = pl.program_id(0)
    n = pl.num_programs(0)
    slot = i % 2
    nrow = tm * top_k

    @pl.when(i == 0)
    def _():
        _start_gather(y_hbm, slot_ref, 0, nrow, buf.at[0], sems.at[0])

    @pl.when(i + 1 < n)
    def _():
        _start_gather(y_hbm, slot_ref, (i + 1) * nrow, nrow, buf.at[1 - slot], sems.at[1 - slot])

    _wait_gather(y_hbm, slot_ref, i * nrow, nrow, buf.at[slot], sems.at[slot])
    acc = x_ref[...]
    r = r_ref[...]
    for k in range(top_k):
        acc = acc + r[:, top_k + k:top_k + k + 1] * buf[slot, pl.ds(k * tm, tm), :]
    o_ref[...] = acc


def _combine(x, route, ys, slots, *, tm=256):
    rows, d = x.shape
    gs = pltpu.PrefetchScalarGridSpec(
        num_scalar_prefetch=1, grid=(rows // tm,),
        in_specs=[pl.BlockSpec((tm, d), lambda i, s: (i, 0)), pl.BlockSpec((tm, LANES), lambda i, s: (i, 0)),
                  pl.BlockSpec(memory_space=pl.ANY)],
        out_specs=pl.BlockSpec((tm, d), lambda i, s: (i, 0)),
        scratch_shapes=[pltpu.VMEM((2, tm * TOP_K, d), F32), pltpu.SemaphoreType.DMA((2,))])
    body = functools.partial(_combine_body, tm=tm, top_k=TOP_K)
    return pl.pallas_call(body, grid_spec=gs, out_shape=jax.ShapeDtypeStruct((rows, d), F32),
                          compiler_params=_params("arbitrary"), name="moe_combine")(slots, x, route, ys)


def _moe_plan(route, n_experts, tm, comb_tm):
    rows = route.shape[0]
    ids = route[:, :TOP_K].astype(jnp.int32)
    flat = ids.reshape(-1)
    n_assign = flat.shape[0]
    order = jnp.argsort(flat, stable=True).astype(jnp.int32)
    rank = jnp.argsort(order).astype(jnp.int32)
    bounds = jnp.searchsorted(flat[order], jnp.arange(n_experts + 1, dtype=jnp.int32)).astype(jnp.int32)
    sort_off = bounds[:-1]
    counts = bounds[1:] - sort_off
    tiles = (counts + tm - 1) // tm
    tile_end = jnp.cumsum(tiles)
    pad_off = (tile_end - tiles) * tm
    t_max = n_assign // tm + n_experts
    n_tiles = tile_end[-1]
    t_idx = jnp.arange(t_max, dtype=jnp.int32)
    tile_e = jnp.searchsorted(tile_end, jnp.minimum(t_idx, n_tiles - 1), side="right").astype(jnp.int32)
    tile_e = jnp.minimum(tile_e, n_experts - 1)
    e_slot = jnp.repeat(tile_e, tm)
    j = jnp.arange(t_max * tm, dtype=jnp.int32) - pad_off[e_slot]
    valid = jnp.logical_and(j < counts[e_slot], jnp.repeat(t_idx < n_tiles, tm))
    src = jnp.clip(sort_off[e_slot] + j, 0, n_assign - 1)
    slot_tok = jnp.where(valid, order[src] // TOP_K, 0)
    slot_of = (pad_off[flat] + rank - sort_off[flat]).reshape(rows, TOP_K)
    slots = slot_of.reshape(rows // comb_tm, comb_tm, TOP_K).transpose(0, 2, 1).reshape(-1)
    return tile_e, n_tiles.reshape(1).astype(jnp.int32), slot_tok, slots


def _hmoe(x, gain, wg, bg, we, be, w1, w3, w2, layer):
    d = x.shape[1]
    n_groups, per_group = we.shape[1], we.shape[2]
    n_experts = n_groups * per_group
    wr = jnp.concatenate([wg, we.reshape(d, n_experts)], axis=1)
    wr = jnp.pad(wr, ((0, 0), (0, LANES - wr.shape[1])))
    wr_hi = wr.astype(BF16)
    wr_lo = (wr - wr_hi.astype(F32)).astype(BF16)
    wcat = jnp.concatenate([wr_hi, wr_lo], axis=1)
    bias = jnp.pad(jnp.concatenate([bg, be.reshape(-1)]), (0, LANES - n_groups - n_experts)).reshape(1, LANES)
    h, route = _router(x, gain, wcat, bias.astype(F32), n_groups=n_groups, per_group=per_group)
    comb_tm = 256
    tile_e, n_tiles, slot_tok, slots = _moe_plan(route, n_experts, MOE_TILE, comb_tm)
    ys = _experts(h, tile_e, n_tiles, slot_tok, w1, w3, w2, layer=layer, tm=MOE_TILE)
    return _combine(x, route, ys, slots, tm=comb_tm)


def _rope_tables(pos, width, first_half_sign):
    half = C_ROPE // 2
    inv = ROPE_THETA ** (-np.arange(half, dtype=np.float64) * 2.0 / C_ROPE)
    ang = np.asarray(pos, np.float64)[:, None] * inv[None, :]
    cos = np.zeros((len(pos), width), np.float32)
    sin = np.zeros((len(pos), width), np.float32)
    cos[:, :half] = np.cos(ang)
    cos[:, half:C_ROPE] = np.cos(ang)
    sin[:, :half] = first_half_sign * np.sin(ang)
    sin[:, half:C_ROPE] = np.sin(ang)
    return cos, sin


def kernel(x_prompt, x_sample, cache_a_k, cache_a_v, state_pool_hist, cache_c_latent, cache_c_kpe, cache_mem_k, cache_mem_v, page_table, mem_prompt, norm_mix, norm_mem, norm_ffn, l0_w_in, l0_w_out, a_gq, a_gk, a_lam, a_g_out, b_w, b_scale, l1_w_in, l1_w_out, c_g_latent, c_w_uk, c_w_uv, c_gq, c_gk, d_g_v, d_w_s, d_b_s, x_wq, x_wk, x_wv, x_gq, x_gk, x_wo, moe_wg, moe_bg, moe_we, moe_be, moe_w1, moe_w3, moe_w2):
    batch, seq, d = x_prompt.shape
    db, t_new, _ = x_sample.shape
    depth = norm_mix.shape[0]
    n_pool, page = cache_a_k.shape[1], cache_a_k.shape[2]
    n_pages = page_table.shape[1]
    past = n_pages * page
    mem_len = mem_prompt.shape[1]
    mp, ms = batch * seq, db * t_new
    m = mp + ms
    a_heads = l0_w_out.shape[1] // 2 // (2 * A_DIM)
    a_group = a_heads // A_KV_HEADS
    a_q = a_heads * 2 * A_DIM
    a_k = A_KV_HEADS * 2 * A_DIM
    b_width = b_scale.shape[1]
    c_latent = c_w_uk.shape[1]
    c_heads = c_w_uk.shape[2]
    c_v = c_w_uv.shape[3]
    d_width = d_g_v.shape[1] * d_g_v.shape[2]
    x_heads = x_wq.shape[2] // X_DIM
    x_width = x_heads * X_DIM
    tq = 256
    nq = seq // tq
    pp = 16

    x = jnp.concatenate([x_prompt.reshape(mp, d), x_sample.reshape(ms, d)], axis=0)
    pos_all = np.concatenate([np.tile(np.arange(seq), batch), np.tile(past + np.arange(t_new), db)])
    cos_all, sin_all = _rope_tables(pos_all, LANES, -1.0)
    key_tab_t = jnp.asarray(np.concatenate(_rope_tables(np.arange(past), C_ROPE, 1.0), axis=1).T.copy())
    new_tab = jnp.asarray(np.concatenate(_rope_tables(past + np.arange(t_new), C_ROPE, 1.0), axis=1))

    ak_l, av_l, pl_l, cl_l, ck_l, mk_l, mv_l, dv_l = [], [], [], [], [], [], [], []

    for layer in range(depth):
        gmix = norm_mix[layer].reshape(1, d)
        if layer % 2 == 0:
            e = layer // 2
            lam_init = 0.8 - 0.6 * math.exp(-0.3 * layer)
            w_in = l0_w_in[e]
            ones64 = _block_ones(256, A_DIM)
            gq_t = jnp.tile(a_gq[e].reshape(1, 2 * A_DIM), (1, 2))
            gk_t = jnp.tile(a_gk[e].reshape(1, 2 * A_DIM), (1, 2))
            (q_pad,) = _matmul([x], [w_in[:, :a_q].astype(BF16)], _ep_diff_q, [(2 * a_q, BF16, 512)],
                               name="l0_q", gain=gmix, aux=[(ones64, "const"), (gq_t, "const")])
            (kvb,) = _matmul([x], [w_in[:, a_q:].astype(BF16)], _ep_diff_kvb, [(2 * a_k + b_width, F32, 256)],
                             name="l0_kvb", gain=gmix, aux=[(ones64, "const"), (gk_t, "const")])
            lam_p = a_lam[e]
            gout = a_g_out[e].reshape(1, 2 * A_DIM)
            n_maps = a_group * 2
            oa = _flash(q_pad, kvb, kvb, name="diff_prompt", grid=(batch, A_KV_HEADS, nq),
                        q_map=lambda b, h, i: (b * nq + i, h), k_map=lambda b, h, i: (b, h),
                        v_map=lambda b, h, i: (b, A_KV_HEADS + h), o_map=lambda b, h, i: (b * nq + i, h),
                        nhk=1, g=n_maps, dq=LANES, dv=LANES, tq=tq, tk=256, skv=seq, causal=True,
                        out_rows=mp, out_cols=a_q, out_dtype=F32, diff=(lam_p, gout, lam_init))
            qs = q_pad[mp:].reshape(db, t_new, A_KV_HEADS, a_group, 2, LANES).transpose(0, 2, 4, 3, 1, 5)
            eye = jnp.eye(A_KV_HEADS, dtype=BF16)[None, :, None, None, None, :, None]
            qbd = (qs[:, :, :, :, :, None, :] * eye).reshape(db, A_KV_HEADS * n_maps * t_new, A_KV_HEADS * LANES)
            ck_t = jnp.transpose(cache_a_k, (0, 1, 3, 4, 5, 2)).reshape(cache_a_k.shape[0], n_pool, a_k, page)
            cv = cache_a_v.reshape(cache_a_v.shape[0], n_pool, page * A_KV_HEADS, 2 * A_DIM)
            oa_s = _diff_sample(qbd, ck_t, cv, e, page_table, kvb, lam_p, gout, lam_init, mp=mp, t_new=t_new, pp=pp)
            b_wb = b_w[e].astype(BF16)
            b_sc = b_scale[e].reshape(1, b_width)
            ob = _pool_prompt(kvb, 2 * a_k, b_wb, b_sc, batch=batch, seq=seq, rows_out=mp)
            hb_s = kvb[mp:, 2 * a_k:].reshape(db, t_new, b_width)
            ext_s = jnp.concatenate([state_pool_hist[e], hb_s], axis=1)
            ob_s = _pool_sample(ext_s, b_wb, b_sc, past=past)
            w_out = l0_w_out[e].astype(BF16)
            (x,) = _matmul([(oa, oa_s), (ob, ob_s)], [w_out[:a_q], w_out[a_q:]], _ep_residual, [(d, F32, 512)],
                           name="l0_out",
                           aux=[(x, "tile")], tn=512)
            kvb_p = kvb[:mp].reshape(batch, seq, -1)
            kvb_s = kvb[mp:].reshape(db, t_new, -1)
            ak_l.append((kvb_p[..., :a_k].reshape(batch, seq, A_KV_HEADS, 2, A_DIM),
                         kvb_s[..., :a_k].reshape(db, t_new, A_KV_HEADS, 2, A_DIM)))
            av_l.append((kvb_p[..., a_k:2 * a_k].reshape(batch, seq, A_KV_HEADS, 2 * A_DIM),
                         kvb_s[..., a_k:2 * a_k].reshape(db, t_new, A_KV_HEADS, 2 * A_DIM)))
            pl_l.append((kvb_p[:, seq - B_HIST:, 2 * a_k:], ext_s[:, t_new:]))
        else:
            o = layer // 2
            w_in = l1_w_in[o]
            o1 = c_heads * C_QK
            o2 = o1 + c_latent
            o3 = o2 + C_ROPE
            o4 = o3 + d_width
            pad_h = C_HEAD_PAD - C_QK
            wq = jnp.pad(w_in[:, :o1].reshape(d, c_heads, C_QK), ((0, 0), (0, 0), (0, pad_h)))
            wq = wq.reshape(d, c_heads * C_HEAD_PAD).astype(BF16)
            gq_pad = jnp.pad(c_gq[o], (0, pad_h)).reshape(1, C_HEAD_PAD)
            gk_pad = jnp.pad(c_gk[o], (0, pad_h)).reshape(1, C_HEAD_PAD)
            rope_aux = lambda g: [(g, "const"), (jnp.asarray(cos_all), "row"), (jnp.asarray(sin_all), "row")]
            (qh,) = _matmul([x], [wq], _ep_head_rope(C_QK ** -0.5, False), [(c_heads * C_HEAD_PAD, BF16, 256)],
                            name="l1_q", gain=gmix, aux=rope_aux(gq_pad))
            wck = jnp.pad(w_in[:, o1:o3], ((0, 0), (0, LANES - C_ROPE))).astype(BF16)
            c_all, kpe_all = _matmul([x], [wck], _ep_latent(c_latent),
                                     [(c_latent, F32, c_latent), (LANES, F32, LANES)], name="l1_ckpe", gain=gmix,
                                     aux=[(c_g_latent[o].reshape(1, c_latent), "const")],
                                     tn=c_latent + LANES)
            (u,) = _matmul([x], [w_in[:, o3:o4].astype(BF16)], _ep_gelu, [(d_width, F32, 256)], name="l1_u",
                           gain=gmix)
            gvw = d_g_v.shape[2]
            (gv,) = _matmul([x], [w_in[:, o4:].astype(BF16)], _ep_gelu_rms, [(d_width, F32, gvw)], name="l1_v",
                            gain=gmix, tn=gvw,
                            aux=[(d_g_v[o].reshape(1, d_width), "col")])
            wuk_pad = jnp.pad(c_w_uk[o], ((0, 0), (0, 0), (0, C_HEAD_PAD - C_NOPE)))
            wuk_pad = wuk_pad.reshape(c_latent, c_heads * C_HEAD_PAD).astype(BF16)
            (kh,) = _matmul([c_all], [wuk_pad], _ep_head_rope(1.0, True), [(c_heads * C_HEAD_PAD, BF16, 256)],
                            name="l1_k", rows=mp,
                            aux=rope_aux(gk_pad) + [(kpe_all, "row")])
            (vh,) = _matmul([c_all], [c_w_uv[o].reshape(c_latent, c_heads * c_v).astype(BF16)], _ep_store,
                            [(c_heads * c_v, BF16, 256)], name="l1_vup", rows=mp)
            tq1 = 512
            nq1 = seq // tq1
            oc = _flash(qh, kh, vh, name="mla_prompt", grid=(batch, c_heads, nq1),
                        q_map=lambda b, h, i: (b * nq1 + i, h), k_map=lambda b, h, i: (b, h),
                        v_map=lambda b, h, i: (b, h), o_map=lambda b, h, i: (b * nq1 + i, h),
                        nhk=1, g=1, dq=C_HEAD_PAD, dv=c_v, tq=tq1, tk=512, skv=seq, causal=True,
                        out_rows=mp, out_cols=c_heads * c_v, out_dtype=BF16)
            wuk_t = jnp.transpose(c_w_uk[o], (1, 2, 0)).astype(BF16)
            gk2 = jnp.tile(c_gk[o][C_NOPE:].reshape(1, C_ROPE), (1, 2))
            qabs, qrot = _absorb(qh, wuk_t, c_gk[o][:C_NOPE].reshape(1, C_NOPE), gk2, mp=mp, db=db, t_new=t_new)
            rows_s = c_heads * t_new
            ckpe_t = jnp.transpose(cache_c_kpe, (0, 1, 3, 2))
            ctx = _mla_sample(qabs.reshape(db, rows_s, c_latent), qrot.reshape(db, rows_s, LANES),
                              wuk_t.reshape(c_heads * C_NOPE, c_latent), key_tab_t,
                              cache_c_latent, ckpe_t, o, page_table, c_all, kpe_all, new_tab,
                              mp=mp, t_new=t_new, pp=pp)
            wuv_h = jnp.transpose(c_w_uv[o], (1, 0, 2)).astype(BF16)
            oc_s = _mla_up(ctx.reshape(db, c_heads, t_new, c_latent), wuv_h)
            w_tri = jnp.tril(d_w_s[o])
            reps = D_CHUNK // t_new
            w_rep = jnp.einsum("ab,gij->gaibj", jnp.eye(reps, dtype=F32), w_tri[:, :t_new, :t_new])
            w_rep = w_rep.reshape(D_GROUPS, D_CHUNK, D_CHUNK)
            gd = d_width // D_GROUPS
            bias_p = jnp.repeat(d_b_s[o].T, gd, axis=1)
            bias_s = jnp.tile(bias_p[:t_new], (reps, 1))
            od = _chunk_gate(u, gv, w_tri.astype(BF16), bias_p, row0=0, rows=mp, name="gate_prompt")
            od_s = _chunk_gate(u, gv, w_rep.astype(BF16), bias_s, row0=mp, rows=ms, name="gate_sample")
            w_out = l1_w_out[o].astype(BF16)
            n_c = c_heads * c_v
            (x,) = _matmul([(oc, oc_s), (od, od_s)], [w_out[:n_c], w_out[n_c:]], _ep_residual, [(d, F32, 512)],
                           name="l1_out",
                           aux=[(x, "tile")], tn=512)
            cl_l.append((c_all[:mp].reshape(batch, seq, c_latent), c_all[mp:].reshape(db, t_new, c_latent)))
            ck_l.append((kpe_all[:mp, :C_ROPE].reshape(batch, seq, C_ROPE),
                         kpe_all[mp:, :C_ROPE].reshape(db, t_new, C_ROPE)))
            dv_l.append(gv[mp:].reshape(db, t_new, d_width))

        ones128 = _block_ones(x_width, X_DIM)
        gq_x = jnp.tile(x_gq[layer].reshape(1, X_DIM), (1, x_heads))
        gk_x = jnp.tile(x_gk[layer].reshape(1, X_DIM), (1, x_heads))
        (qx,) = _matmul([x], [x_wq[layer].astype(BF16)], _ep_segnorm(X_DIM, X_DIM ** -0.5), [(x_width, F32, x_width)],
                        name="x_q", gain=norm_mem[layer].reshape(1, d), tn=x_width,
                        aux=[(ones128, "const"), (gq_x, "const")])
        memf = mem_prompt.reshape(batch * mem_len, d)
        (mkp,) = _matmul([memf], [x_wk[layer].astype(BF16)], _ep_segnorm(X_DIM, 1.0), [(x_width, F32, x_width)],
                         name="x_k", tn=x_width, aux=[(ones128, "const"), (gk_x, "const")])
        (mvp,) = _matmul([memf], [x_wv[layer].astype(BF16)], _ep_store, [(x_width, F32, x_width)], name="x_v",
                         tn=x_width)
        ox = _flash(qx, mkp, mvp, name="cross_prompt", grid=(batch, 1, nq),
                    q_map=lambda b, h, i: (b * nq + i, 0), k_map=lambda b, h, i: (b, 0),
                    v_map=lambda b, h, i: (b, 0), o_map=lambda b, h, i: (b * nq + i, 0),
                    nhk=x_heads, g=1, dq=X_DIM, dv=X_DIM, tq=tq, tk=mem_len, skv=mem_len, causal=False,
                    out_rows=mp, out_cols=x_width, out_dtype=F32)
        cmk = cache_mem_k.reshape(depth * db * mem_len * x_heads, X_DIM)
        cmv = cache_mem_v.reshape(depth * db * mem_len * x_heads, X_DIM)
        ox_s = _cross_sample(qx, cmk, cmv, layer=layer, mp=mp, db=db, t_new=t_new, n_heads=x_heads, mem_len=mem_len)
        (x,) = _matmul([(ox, ox_s)], [x_wo[layer].astype(BF16)], _ep_residual, [(d, F32, 512)], name="x_out",
                       aux=[(x, "tile")], tn=512)
        mk_l.append(mkp.reshape(batch, mem_len, x_heads, X_DIM))
        mv_l.append(mvp.reshape(batch, mem_len, x_heads, X_DIM))

        x = _hmoe(x, norm_ffn[layer].reshape(1, d), moe_wg[layer], moe_bg[layer], moe_we[layer], moe_be[layer],
                  moe_w1, moe_w3, moe_w2, layer)

    stack = lambda items, k: jnp.stack([it[k] for it in items])
    return (x[:mp].reshape(batch, seq, d), x[mp:].reshape(db, t_new, d),
            stack(ak_l, 0), stack(av_l, 0), stack(pl_l, 0), stack(cl_l, 0), stack(ck_l, 0),
            jnp.stack(mk_l), jnp.stack(mv_l),
            stack(ak_l, 1), stack(av_l, 1), stack(pl_l, 1), stack(cl_l, 1), stack(ck_l, 1), jnp.stack(dv_l))
```
